```python
import jax
import jax.numpy as jnp
from jax import lax
import numpy as np


D_MODEL = 1024
BATCH = 2
SEQ = 16384
DEPTH = 2

CTX_LEN = 256
GRID_W = 64
GLA_HEADS = 4
GLA_KEY = D_MODEL // 2
GLA_VAL = D_MODEL
GLA_DK = GLA_KEY // GLA_HEADS
GLA_DV = GLA_VAL // GLA_HEADS
GLA_RANK = 16
GLA_CHUNK = 64
GLA_GATE_NORM = 16.0
CONV_D = D_MODEL
CONV_K = 31
SGU_D = D_MODEL
SGU_GROUPS = 8
SGU_GC = SGU_D // SGU_GROUPS
SGU_CHUNK = 128
N_BRANCH = 3
D_FF = ((8 * D_MODEL + 3 * 256 - 1) // (3 * 256)) * 256
IN_SPLITS = (GLA_KEY, GLA_KEY, GLA_VAL, 2 * GLA_RANK, GLA_VAL, 2 * CONV_D, 2 * SGU_D, N_BRANCH * D_MODEL)
IN_COLS = sum(IN_SPLITS)
EPS = 1e-6

kernel_name = 'hybrid_gla_conv_sgu_dit_block'


def rms_norm(x, g):
    xf = x.astype(jnp.float32)
    y = xf * lax.rsqrt(jnp.mean(xf * xf, axis=-1, keepdims=True) + EPS)
    return (y * g.astype(jnp.float32)).astype(x.dtype)


def layer_norm(x, g, b):
    xf = x.astype(jnp.float32)
    xc = xf - jnp.mean(xf, axis=-1, keepdims=True)
    y = xc * lax.rsqrt(jnp.mean(xc * xc, axis=-1, keepdims=True) + EPS)
    return (y * g.astype(jnp.float32) + b.astype(jnp.float32)).astype(x.dtype)


def modulate(h, shift, scale):
    return h * (1.0 + scale) + shift


def flip_seq(t):
    return jnp.flip(t, axis=1)


def gla_log_gates(a_dn, a_up, a_b):
    bsz, L = a_dn.shape[:2]
    z = jnp.einsum('bler,erk->blek', a_dn.reshape(bsz, L, 2, GLA_RANK), a_up) + a_b
    lg = jax.nn.log_sigmoid(z.astype(jnp.float32)) / GLA_GATE_NORM
    lg = lg.reshape(bsz, L, 2, GLA_HEADS, GLA_DK)
    return lg[:, :, 0], lg[:, :, 1]


def gla_chunked(q, k, v, logg, s0):
    bsz, L, H, DK = q.shape
    DV = v.shape[-1]
    n = L // GLA_CHUNK
    q = q.reshape(bsz, n, GLA_CHUNK, H, DK)
    k = k.reshape(bsz, n, GLA_CHUNK, H, DK)
    logg = logg.reshape(bsz, n, GLA_CHUNK, H, DK)
    v = v.reshape(bsz, n, GLA_CHUNK, H, DV)
    b = jnp.cumsum(logg, axis=2)
    b_last = b[:, :, -1:]
    q_e = q * jnp.exp(b)
    a = jnp.einsum('bnihd,bnjhd->bnhij', q_e, k * jnp.exp(-b))
    a = jnp.where(jnp.tril(jnp.ones((GLA_CHUNK, GLA_CHUNK), dtype=bool)), a, 0.0)
    o_intra = jnp.einsum('bnhij,bnjhv->bnihv', a, v)
    k_dec = k * jnp.exp(b_last - b)

    def step(s, xs):
        q_c, k_c, v_c, dec_c = xs
        o_c = jnp.einsum('bchd,bhdv->bchv', q_c, s)
        s = dec_c[..., None] * s + jnp.einsum('bchd,bchv->bhdv', k_c, v_c)
        return s, o_c

    xs = (jnp.moveaxis(q_e, 1, 0), jnp.moveaxis(k_dec, 1, 0), jnp.moveaxis(v, 1, 0),
          jnp.moveaxis(jnp.exp(b_last[:, :, 0]), 1, 0))
    s_fin, o_inter = lax.scan(step, s0, xs)
    o = o_intra + jnp.moveaxis(o_inter, 0, 1)
    return o.reshape(bsz, L, H, DV), s_fin


def gla_bidir(q, k, v, lg_f, lg_b, s0_f, s0_b):
    o_f, s_f = gla_chunked(q, k, v, lg_f, s0_f)
    o_b, s_b = gla_chunked(flip_seq(q), flip_seq(k), flip_seq(v), flip_seq(lg_b), s0_b)
    diag = jnp.sum(q * k, axis=-1, keepdims=True) * v
    return o_f + flip_seq(o_b) - diag, s_f, s_b


def gla_ctx_state(k, v, logg):
    b = jnp.cumsum(logg, axis=1)
    return jnp.einsum('blhd,blhv->bhdv', k * jnp.exp(b[:, -1:] - b), v)


def depthwise_conv(h, w, bias):
    pad = CONV_K // 2
    y = lax.conv_general_dilated(h, w.astype(h.dtype)[:, None, :], window_strides=(1,),
                                 padding=[(pad, pad)], dimension_numbers=('NWC', 'WIO', 'NWC'),
                                 feature_group_count=h.shape[-1])
    return y + bias


def swiglu(h, w1, w2):
    g, u = jnp.split(h @ w1, 2, axis=-1)
    return (jax.nn.silu(g) * u) @ w2


def mixer(p, lp, s0_f, s0_b, rows):
    bsz, L = p.shape[:2]
    dt = p.dtype
    f32 = jnp.float32
    q, k, v, a_dn, r, conv_in, sgu_in, gate_logit = jnp.split(
        p, np.cumsum(IN_SPLITS)[:-1].tolist(), axis=-1)

    qh = q.astype(f32).reshape(bsz, L, GLA_HEADS, GLA_DK) * (GLA_DK ** -0.5)
    kh = k.astype(f32).reshape(bsz, L, GLA_HEADS, GLA_DK)
    vh = v.astype(f32).reshape(bsz, L, GLA_HEADS, GLA_DV)
    lg_f, lg_b = gla_log_gates(a_dn, lp['gla_a_up'], lp['gla_a_b'])
    o, s_f, s_b = gla_bidir(qh, kh, vh, lg_f, lg_b, s0_f, s0_b)
    o = o * lax.rsqrt(jnp.mean(o * o, axis=-1, keepdims=True) + EPS)
    o = o.reshape(bsz, L, GLA_VAL) * lp['gla_norm_g'].astype(f32) * jax.nn.silu(r.astype(f32))
    y_a = o.astype(dt) @ lp['w_o_gla']

    c1, c2 = jnp.split(conv_in, 2, axis=-1)
    hc = c1 * jax.nn.sigmoid(c2)
    if rows is None:
        hc = depthwise_conv(hc, lp['conv_w'], lp['conv_b'])
    else:
        hc = depthwise_conv(hc.reshape(bsz * rows, GRID_W, CONV_D), lp['conv_w'],
                            lp['conv_b']).reshape(bsz, L, CONV_D)
    y_b = jax.nn.silu(layer_norm(hc, lp['conv_ln_g'], lp['conv_ln_b'])) @ lp['w_o_conv']

    su, sv = jnp.split(jax.nn.gelu(sgu_in), 2, axis=-1)
    sv = layer_norm(sv, lp['sgu_ln_g'], lp['sgu_ln_b']).reshape(
        bsz, L // SGU_CHUNK, SGU_CHUNK, SGU_GROUPS, SGU_GC)
    sp = jnp.einsum('gij,bnjgc->bnigc', lp['sgu_ws'], sv) + lp['sgu_b'].T[:, :, None]
    y_c = (su * sp.reshape(bsz, L, SGU_D)) @ lp['w_o_sgu']

    gates = jax.nn.sigmoid(gate_logit.astype(f32)).astype(dt).reshape(bsz, L, N_BRANCH, D_MODEL)
    y = gates[:, :, 0] * y_a + gates[:, :, 1] * y_b + gates[:, :, 2] * y_c
    return y @ lp['w_out'], s_f, s_b


def setup_inputs(seed: int = 0) -> dict:
    key = jax.random.key(seed)
    ks = iter(jax.random.split(key, 32))

    def nrm(shape, scale):
        return jax.random.normal(next(ks), shape, jnp.float32) * scale

    def gain(shape):
        return 1.0 + nrm(shape, 0.02)

    return {
        'x': nrm((BATCH, SEQ, D_MODEL), 1.0),
        'c': nrm((BATCH, D_MODEL), 1.0),
        'ctx': nrm((BATCH, CTX_LEN, D_MODEL), 1.0),
        'c_ctx': nrm((D_MODEL,), 1.0),
        'w_ada': nrm((DEPTH, D_MODEL, 6 * D_MODEL), D_MODEL ** -0.5),
        'b_ada': nrm((DEPTH, 6 * D_MODEL), 0.02),
        'norm1_g': gain((DEPTH, D_MODEL)),
        'norm2_g': gain((DEPTH, D_MODEL)),
        'w_in': nrm((DEPTH, D_MODEL, IN_COLS), D_MODEL ** -0.5),
        'gla_a_up': nrm((DEPTH, 2, GLA_RANK, GLA_KEY), GLA_RANK ** -0.5),
        'gla_a_b': nrm((DEPTH, 2, GLA_KEY), 0.1),
        'gla_norm_g': gain((DEPTH, GLA_VAL)),
        'w_o_gla': nrm((DEPTH, GLA_VAL, D_MODEL), GLA_VAL ** -0.5),
        'conv_w': nrm((DEPTH, CONV_K, CONV_D), CONV_K ** -0.5),
        'conv_b': nrm((DEPTH, CONV_D), 0.02),
        'conv_ln_g': gain((DEPTH, CONV_D)),
        'conv_ln_b': nrm((DEPTH, CONV_D), 0.02),
        'w_o_conv': nrm((DEPTH, CONV_D, D_MODEL), CONV_D ** -0.5),
        'sgu_ln_g': gain((DEPTH, SGU_D)),
        'sgu_ln_b': nrm((DEPTH, SGU_D), 0.02),
        'sgu_ws': nrm((DEPTH, SGU_GROUPS, SGU_CHUNK, SGU_CHUNK), SGU_CHUNK ** -0.5),
        'sgu_b': gain((DEPTH, SGU_GROUPS, SGU_CHUNK)),
        'w_o_sgu': nrm((DEPTH, SGU_D, D_MODEL), SGU_D ** -0.5),
        'w_out': nrm((DEPTH, D_MODEL, D_MODEL), D_MODEL ** -0.5),
        'w_ffn_in': nrm((DEPTH, D_MODEL, 2 * D_FF), D_MODEL ** -0.5),
        'w_ffn_out': nrm((DEPTH, D_FF, D_MODEL), D_FF ** -0.5),
        'final_g': gain((D_MODEL,)),
    }


def reference(x, c, ctx, c_ctx, w_ada, b_ada, norm1_g, norm2_g, w_in, gla_a_up, gla_a_b,
              gla_norm_g, w_o_gla, conv_w, conv_b, conv_ln_g, conv_ln_b, w_o_conv, sgu_ln_g,
              sgu_ln_b, sgu_ws, sgu_b, w_o_sgu, w_out, w_ffn_in, w_ffn_out, final_g):
    bsz = x.shape[0]
    rows = x.shape[1] // GRID_W
    h_lat = x
    h_ctx = ctx
    silu_c = jax.nn.silu(c)[:, None, :]
    silu_cc = jax.nn.silu(c_ctx)[None, None, :]
    zero_state = jnp.zeros((bsz, GLA_HEADS, GLA_DK, GLA_DV), jnp.float32)
    for l in range(DEPTH):
        lp = {'gla_a_up': gla_a_up[l], 'gla_a_b': gla_a_b[l], 'gla_norm_g': gla_norm_g[l],
              'w_o_gla': w_o_gla[l], 'conv_w': conv_w[l], 'conv_b': conv_b[l],
              'conv_ln_g': conv_ln_g[l], 'conv_ln_b': conv_ln_b[l], 'w_o_conv': w_o_conv[l],
              'sgu_ln_g': sgu_ln_g[l], 'sgu_ln_b': sgu_ln_b[l], 'sgu_ws': sgu_ws[l],
              'sgu_b': sgu_b[l], 'w_o_sgu': w_o_sgu[l], 'w_out': w_out[l]}
        sh1, sc1, g1, sh2, sc2, g2 = jnp.split(silu_c @ w_ada[l] + b_ada[l], 6, axis=-1)
        csh1, csc1, cg1, csh2, csc2, cg2 = jnp.split(silu_cc @ w_ada[l] + b_ada[l], 6, axis=-1)

        hc = modulate(rms_norm(h_ctx, norm1_g[l]), csh1, csc1)
        if l == DEPTH - 1:
            kva = hc @ w_in[l][:, GLA_KEY:2 * GLA_KEY + GLA_VAL + 2 * GLA_RANK]
            kc, vc, ac = jnp.split(kva, [GLA_KEY, GLA_KEY + GLA_VAL], axis=-1)
            kc = kc.astype(jnp.float32).reshape(bsz, -1, GLA_HEADS, GLA_DK)
            vc = vc.astype(jnp.float32).reshape(bsz, -1, GLA_HEADS, GLA_DV)
            lgc_f, lgc_b = gla_log_gates(ac, gla_a_up[l], gla_a_b[l])
            s_f = gla_ctx_state(kc, vc, lgc_f)
            s_b = gla_ctx_state(flip_seq(kc), flip_seq(vc), flip_seq(lgc_b))
        else:
            yc, s_f, s_b = mixer(hc @ w_in[l], lp, zero_state, zero_state, None)
            h_ctx = h_ctx + cg1 * yc
            h_ctx = h_ctx + cg2 * swiglu(modulate(rms_norm(h_ctx, norm2_g[l]), csh2, csc2),
                                         w_ffn_in[l], w_ffn_out[l])

        hl = modulate(rms_norm(h_lat, norm1_g[l]), sh1, sc1)
        yl, _, _ = mixer(hl @ w_in[l], lp, s_f, s_b, rows)
        h_lat = h_lat + g1 * yl
        h_lat = h_lat + g2 * swiglu(modulate(rms_norm(h_lat, norm2_g[l]), sh2, sc2),
                                    w_ffn_in[l], w_ffn_out[l])
    return rms_norm(h_lat, final_g)
```

```python
import functools

import jax
import jax.numpy as jnp
from jax import lax
from jax.experimental import pallas as pl
from jax.experimental.pallas import tpu as pltpu

F32 = jnp.float32
BF16 = jnp.bfloat16

D_MODEL = 1024
GRID_W = 64
GLA_HEADS = 4
GLA_KEY = D_MODEL // 2
GLA_VAL = D_MODEL
GLA_DK = GLA_KEY // GLA_HEADS
GLA_DV = GLA_VAL // GLA_HEADS
GLA_RANK = 16
GLA_CHUNK = 64
GLA_GATE_NORM = 16.0
CONV_K = 31
CONV_PAD = CONV_K // 2
SGU_GROUPS = 8
SGU_CHUNK = 128
D_FF = ((8 * D_MODEL + 3 * 256 - 1) // (3 * 256)) * 256
EPS = 1e-6

LANES = 128
QKV_COLS = 2 * GLA_KEY + GLA_VAL
ADN_PAD = LANES
CONV_HALO = 16
VMEM_LIMIT = 56 * 1024 * 1024


def _dot(a, b):
    return jnp.dot(a, b, preferred_element_type=F32)


def _dot_nt(a, b):
    return lax.dot_general(a, b, (((1,), (1,)), ((), ())), preferred_element_type=F32)


def _dot_tn(a, b):
    return lax.dot_general(a, b, (((0,), (0,)), ((), ())), preferred_element_type=F32)


def _rms_mod(h, g, shift, scale):
    y = h * lax.rsqrt(jnp.mean(h * h, axis=-1, keepdims=True) + EPS) * g
    return y * (1.0 + scale) + shift


def _layer_norm(x, g, b):
    xc = x - jnp.mean(x, axis=-1, keepdims=True)
    y = xc * lax.rsqrt(jnp.mean(xc * xc, axis=-1, keepdims=True) + EPS)
    return y * g + b


def _sigmoid(x):
    return jax.nn.sigmoid(x)


def _silu(x):
    return x * jax.nn.sigmoid(x)


def _log_sigmoid(z):
    return jnp.minimum(z, 0.0) - jnp.log1p(jnp.exp(-jnp.abs(z)))


def _ada_kernel(c_ref, w_ref, b_ref, o_ref):
    o_ref[...] = _dot(_silu(c_ref[...]), w_ref[...]) + b_ref[...]


def _ada(cond, w_ada, b_ada):
    depth, d, n = w_ada.shape
    bn = n // 4
    return pl.pallas_call(
        _ada_kernel,
        grid=(depth, n // bn),
        in_specs=[
            pl.BlockSpec((8, d), lambda l, j: (0, 0)),
            pl.BlockSpec((None, d, bn), lambda l, j: (l, 0, j)),
            pl.BlockSpec((None, 1, bn), lambda l, j: (l, 0, j)),
        ],
        out_specs=pl.BlockSpec((None, 8, bn), lambda l, j: (l, 0, j)),
        out_shape=jax.ShapeDtypeStruct((depth, 8, n), F32),
        compiler_params=pltpu.CompilerParams(vmem_limit_bytes=VMEM_LIMIT),
        name="ada",
    )(cond, w_ada, b_ada.reshape(depth, 1, n))


def _gla_tile(q, k, v, lg, s_ref, o_ref, *, reverse, n_tok):
    c = GLA_CHUNK
    n_chunks = n_tok // c
    row = lax.broadcasted_iota(jnp.int32, (c, c), 0)
    col = lax.broadcasted_iota(jnp.int32, (c, c), 1)
    if reverse:
        tri = (col >= row).astype(BF16)
        keep = col > row
    else:
        tri = (col <= row).astype(BF16)
        keep = col <= row
    order = range(n_chunks - 1, -1, -1) if reverse else range(n_chunks)
    for ci in order:
        rows = slice(ci * c, (ci + 1) * c)
        lgc = lg[rows]
        lg_hi = lgc.astype(BF16)
        lg_lo = (lgc - lg_hi.astype(F32)).astype(BF16)
        b = _dot(tri, lg_hi) + _dot(tri, lg_lo)
        b_last = b[0:1] if reverse else b[c - 1:c]
        kc = k[rows]
        qe = (q[rows] * jnp.exp(b)).astype(BF16)
        km = (kc * jnp.exp(-b)).astype(BF16)
        kd = (kc * jnp.exp(b_last - b)).astype(BF16)
        dec = jnp.exp(b_last)
        for hh in range(GLA_HEADS):
            ksl = slice(hh * GLA_DK, (hh + 1) * GLA_DK)
            vsl = slice(hh * GLA_DV, (hh + 1) * GLA_DV)
            vh = v[rows, vsl]
            a = jnp.where(keep, _dot_nt(qe[:, ksl], km[:, ksl]), 0.0).astype(BF16)
            st = s_ref[hh]
            o_ref[rows, vsl] = _dot(a, vh) + _dot_nt(qe[:, ksl], st.astype(BF16))
            s_ref[hh] = st * dec[:, ksl] + _dot_tn(vh, kd[:, ksl])


def _gla_bwd_kernel(h_ref, mod_ref, g_ref, w_ref, aup_ref, ab_ref, s0_ref,
                    q_ref, k_ref, v_ref, lgf_ref, ob_ref, sfin_ref, s_scr, *, n_tok):
    t = pl.program_id(1)

    @pl.when(t == 0)
    def _():
        s_scr[...] = s0_ref[...]

    hm = _rms_mod(h_ref[...], g_ref[...], mod_ref[0:1, :], mod_ref[1:2, :]).astype(BF16)
    p = _dot(hm, w_ref[...])
    q = p[:, 0:GLA_KEY] * (GLA_DK ** -0.5)
    k = p[:, GLA_KEY:2 * GLA_KEY]
    v = p[:, 2 * GLA_KEY:QKV_COLS].astype(BF16)
    a_dn = p[:, QKV_COLS:QKV_COLS + ADN_PAD].astype(BF16)
    z = _dot(a_dn, aup_ref[...]) + ab_ref[...]
    lg = _log_sigmoid(z) * (1.0 / GLA_GATE_NORM)
    q_ref[...] = q.astype(BF16)
    k_ref[...] = k.astype(BF16)
    v_ref[...] = v
    lgf_ref[...] = lg[:, 0:GLA_KEY]
    _gla_tile(q, k, v, lg[:, GLA_KEY:2 * GLA_KEY], s_scr, ob_ref, reverse=True, n_tok=n_tok)

    @pl.when(t == pl.num_programs(1) - 1)
    def _():
        sfin_ref[...] = s_scr[...]


def _gla_fwd_kernel(q_ref, k_ref, v_ref, lgf_ref, ob_ref, s0_ref, on_ref, sfin_ref, s_scr, o_scr, *, n_tok):
    t = pl.program_id(1)

    @pl.when(t == 0)
    def _():
        s_scr[...] = s0_ref[...]

    _gla_tile(q_ref[...].astype(F32), k_ref[...].astype(F32), v_ref[...], lgf_ref[...], s_scr, o_scr,
              reverse=False, n_tok=n_tok)
    for hh in range(GLA_HEADS):
        vsl = slice(hh * GLA_DV, (hh + 1) * GLA_DV)
        o = o_scr[:, vsl] + ob_ref[:, vsl]
        on_ref[:, vsl] = o * lax.rsqrt(jnp.mean(o * o, axis=-1, keepdims=True) + EPS)

    @pl.when(t == pl.num_programs(1) - 1)
    def _():
        sfin_ref[...] = s_scr[...]


def _const_spec(shape):
    nd = len(shape)
    return pl.BlockSpec(shape, lambda b, t: (0,) * nd, pipeline_mode=pl.Buffered(1))


def _state_spec():
    return pl.BlockSpec((None, GLA_HEADS, GLA_DV, GLA_DK), lambda b, t: (b, 0, 0, 0))


def _gla_bwd(h, mod, g, w_qkva, aup, ab, s0, *, n_tok):
    bsz, seq, d = h.shape
    n_t = seq // n_tok
    rev = lambda b, t: (b, n_t - 1 - t, 0)
    tok = lambda w: pl.BlockSpec((None, n_tok, w), rev)
    return pl.pallas_call(
        functools.partial(_gla_bwd_kernel, n_tok=n_tok),
        grid=(bsz, n_t),
        in_specs=[
            tok(d),
            pl.BlockSpec((None, 8, d), lambda b, t: (b, 0, 0)),
            _const_spec((1, d)),
            _const_spec(w_qkva.shape),
            _const_spec(aup.shape),
            _const_spec(ab.shape),
            _state_spec(),
        ],
        out_specs=[tok(GLA_KEY), tok(GLA_KEY), tok(GLA_VAL), tok(GLA_KEY), tok(GLA_VAL), _state_spec()],
        out_shape=[
            jax.ShapeDtypeStruct((bsz, seq, GLA_KEY), BF16),
            jax.ShapeDtypeStruct((bsz, seq, GLA_KEY), BF16),
            jax.ShapeDtypeStruct((bsz, seq, GLA_VAL), BF16),
            jax.ShapeDtypeStruct((bsz, seq, GLA_KEY), F32),
            jax.ShapeDtypeStruct((bsz, seq, GLA_VAL), F32),
            jax.ShapeDtypeStruct((bsz, GLA_HEADS, GLA_DV, GLA_DK), F32),
        ],
        scratch_shapes=[pltpu.VMEM((GLA_HEADS, GLA_DV, GLA_DK), F32)],
        compiler_params=pltpu.CompilerParams(
            dimension_semantics=("arbitrary", "arbitrary"), vmem_limit_bytes=VMEM_LIMIT),
        name="gla_bwd",
    )(h, mod, g, w_qkva, aup, ab, s0)


def _gla_fwd(q, k, v, lgf, ob, s0, *, n_tok):
    bsz, seq, _ = v.shape
    n_t = seq // n_tok
    tok = lambda w: pl.BlockSpec((None, n_tok, w), lambda b, t: (b, t, 0))
    return pl.pallas_call(
        functools.partial(_gla_fwd_kernel, n_tok=n_tok),
        grid=(bsz, n_t),
        in_specs=[tok(GLA_KEY), tok(GLA_KEY), tok(GLA_VAL), tok(GLA_KEY), tok(GLA_VAL), _state_spec()],
        out_specs=[tok(GLA_VAL), _state_spec()],
        out_shape=[
            jax.ShapeDtypeStruct((bsz, seq, GLA_VAL), F32),
            jax.ShapeDtypeStruct((bsz, GLA_HEADS, GLA_DV, GLA_DK), F32),
        ],
        scratch_shapes=[pltpu.VMEM((GLA_HEADS, GLA_DV, GLA_DK), F32), pltpu.VMEM((n_tok, GLA_VAL), F32)],
        compiler_params=pltpu.CompilerParams(
            dimension_semantics=("arbitrary", "arbitrary"), vmem_limit_bytes=VMEM_LIMIT),
        name="gla_fwd",
    )(q, k, v, lgf, ob, s0)


def _mixer_kernel(h_ref, on_ref, mod_ref, g_ref, w_r_ref, w_conv_ref, w_sgu_ref, w_gate_ref,
                  gla_g_ref, w_o_gla_ref, cw_ref, cb_ref, cln_g_ref, cln_b_ref, w_o_conv_ref,
                  sln_g_ref, sln_b_ref, ws_ref, sb_ref, w_o_sgu_ref, w_out_ref,
                  out_ref, cpad, cout, sp_scr, *, n_tok, seg):
    h = h_ref[...]
    hm = _rms_mod(h, g_ref[...], mod_ref[0:1, :], mod_ref[1:2, :]).astype(BF16)

    r = _dot(hm, w_r_ref[...])
    y_a = _dot((on_ref[...] * gla_g_ref[...] * _silu(r)).astype(BF16), w_o_gla_ref[...])

    cin = _dot(hm, w_conv_ref[...])
    hc = cin[:, 0:D_MODEL] * _sigmoid(cin[:, D_MODEL:2 * D_MODEL])
    n_seg = n_tok // seg
    zeros = jnp.zeros((CONV_HALO, D_MODEL), F32)
    for s in range(n_seg):
        cpad[s, 0:CONV_HALO, :] = zeros
        cpad[s, CONV_HALO:CONV_HALO + seg, :] = hc[s * seg:(s + 1) * seg]
        cpad[s, CONV_HALO + seg:2 * CONV_HALO + seg, :] = zeros
    sub = 64
    off0 = CONV_HALO - CONV_PAD

    def conv_seg(s, carry):
        for r0 in range(0, seg, sub):
            for l0 in range(0, D_MODEL, LANES):
                acc = jnp.zeros((sub, LANES), F32)
                for j in range(CONV_K):
                    acc = acc + cw_ref[j:j + 1, l0:l0 + LANES] * cpad[s, pl.ds(off0 + r0 + j, sub), l0:l0 + LANES]
                cout[s, r0:r0 + sub, l0:l0 + LANES] = acc
        return carry

    lax.fori_loop(0, n_seg, conv_seg, 0)
    conv = cout[...].reshape(n_tok, D_MODEL) + cb_ref[...]
    y_b = _dot(_silu(_layer_norm(conv, cln_g_ref[...], cln_b_ref[...])).astype(BF16), w_o_conv_ref[...])

    sg = jax.nn.gelu(_dot(hm, w_sgu_ref[...]))
    su = sg[:, 0:D_MODEL]
    sv = _layer_norm(sg[:, D_MODEL:2 * D_MODEL], sln_g_ref[...], sln_b_ref[...]).astype(BF16)
    gc = D_MODEL // SGU_GROUPS
    for n in range(n_tok // SGU_CHUNK):
        rows = slice(n * SGU_CHUNK, (n + 1) * SGU_CHUNK)
        for gi in range(SGU_GROUPS):
            lsl = slice(gi * gc, (gi + 1) * gc)
            sp_scr[rows, lsl] = _dot(ws_ref[gi], sv[rows, lsl]) + sb_ref[:, lsl]
    y_c = _dot((su * sp_scr[...]).astype(BF16), w_o_sgu_ref[...])

    gates = _sigmoid(_dot(hm, w_gate_ref[...]))
    y = (gates[:, 0:D_MODEL] * y_a + gates[:, D_MODEL:2 * D_MODEL] * y_b
         + gates[:, 2 * D_MODEL:3 * D_MODEL] * y_c)
    out_ref[...] = h + mod_ref[2:3, :] * _dot(y.astype(BF16), w_out_ref[...])


def _mixer(h, on, mod, g, wts, *, n_tok, seg):
    bsz, seq, d = h.shape
    n_t = seq // n_tok
    tok = pl.BlockSpec((None, n_tok, d), lambda b, t: (b, t, 0))
    return pl.pallas_call(
        functools.partial(_mixer_kernel, n_tok=n_tok, seg=seg),
        grid=(bsz, n_t),
        in_specs=[tok, tok, pl.BlockSpec((None, 8, d), lambda b, t: (b, 0, 0)), _const_spec((1, d))]
        + [_const_spec(w.shape) for w in wts],
        out_specs=tok,
        out_shape=jax.ShapeDtypeStruct((bsz, seq, d), F32),
        scratch_shapes=[
            pltpu.VMEM((n_tok // seg, seg + 2 * CONV_HALO, d), F32),
            pltpu.VMEM((n_tok // seg, seg, d), F32),
            pltpu.VMEM((n_tok, d), F32),
        ],
        compiler_params=pltpu.CompilerParams(
            dimension_semantics=("parallel", "parallel"), vmem_limit_bytes=VMEM_LIMIT),
        name="mixer",
    )(h, on, mod, g, *wts)


def _ffn_kernel(h_ref, mod_ref, g_ref, w1_ref, w2_ref, out_ref):
    h = h_ref[...]
    hm = _rms_mod(h, g_ref[...], mod_ref[3:4, :], mod_ref[4:5, :]).astype(BF16)
    gu = _dot(hm, w1_ref[...])
    act = (_silu(gu[:, 0:D_FF]) * gu[:, D_FF:2 * D_FF]).astype(BF16)
    out_ref[...] = h + mod_ref[5:6, :] * _dot(act, w2_ref[...])


def _ffn(h, mod, g, w1, w2, *, n_tok):
    bsz, seq, d = h.shape
    tok = pl.BlockSpec((None, n_tok, d), lambda b, t: (b, t, 0))
    return pl.pallas_call(
        functools.partial(_ffn_kernel),
        grid=(bsz, seq // n_tok),
        in_specs=[tok, pl.BlockSpec((None, 8, d), lambda b, t: (b, 0, 0)), _const_spec((1, d)),
                  _const_spec(w1.shape), _const_spec(w2.shape)],
        out_specs=tok,
        out_shape=jax.ShapeDtypeStruct((bsz, seq, d), F32),
        compiler_params=pltpu.CompilerParams(
            dimension_semantics=("parallel", "parallel"), vmem_limit_bytes=VMEM_LIMIT),
        name="ffn",
    )(h, mod, g, w1, w2)


def _final_norm_kernel(h_ref, g_ref, o_ref):
    h = h_ref[...]
    o_ref[...] = h * lax.rsqrt(jnp.mean(h * h, axis=-1, keepdims=True) + EPS) * g_ref[...]


def _final_norm(h, g, *, n_tok):
    bsz, seq, d = h.shape
    tok = pl.BlockSpec((None, n_tok, d), lambda b, t: (b, t, 0))
    return pl.pallas_call(
        _final_norm_kernel,
        grid=(bsz, seq // n_tok),
        in_specs=[tok, _const_spec((1, d))],
        out_specs=tok,
        out_shape=jax.ShapeDtypeStruct((bsz, seq, d), F32),
        compiler_params=pltpu.CompilerParams(dimension_semantics=("parallel", "parallel")),
        name="final_norm",
    )(h, g)


def _pick_tile(seq, want):
    return min(seq, want)


def kernel(x, c, ctx, c_ctx, w_ada, b_ada, norm1_g, norm2_g, w_in, gla_a_up, gla_a_b, gla_norm_g, w_o_gla,
           conv_w, conv_b, conv_ln_g, conv_ln_b, w_o_conv, sgu_ln_g, sgu_ln_b, sgu_ws, sgu_b, w_o_sgu, w_out,
           w_ffn_in, w_ffn_out, final_g):
    bsz, seq, d = x.shape
    depth = w_ada.shape[0]
    ctx_len = ctx.shape[1]

    cond = jnp.zeros((8, d), F32).at[0:bsz].set(c).at[bsz].set(c_ctx)
    ada = _ada(cond, w_ada, b_ada).reshape(depth, 8, 6, d)
    ada = jnp.pad(ada, ((0, 0), (0, 0), (0, 2), (0, 0)))

    row = lambda a: a.reshape(1, -1)
    c0 = 0
    offs = []
    for wdt in (GLA_KEY, GLA_KEY, GLA_VAL, 2 * GLA_RANK, GLA_VAL, 2 * D_MODEL, 2 * D_MODEL, 3 * D_MODEL):
        offs.append((c0, c0 + wdt))
        c0 += wdt
    (_, _), (_, _), (_, v_end), (a_lo, a_hi), (r_lo, r_hi), (cv_lo, cv_hi), (sg_lo, sg_hi), (gt_lo, gt_hi) = offs

    h_lat, h_ctx = x, ctx
    zero_state = jnp.zeros((bsz, GLA_HEADS, GLA_DV, GLA_DK), F32)
    t_lat = _pick_tile(seq, 512)
    t_mix = _pick_tile(seq, 256)
    for l in range(depth):
        wl = w_in[l]
        w_qkva = jnp.pad(wl[:, 0:a_hi], ((0, 0), (0, ADN_PAD - (a_hi - a_lo)))).astype(BF16)
        aup = jnp.zeros((ADN_PAD, 2 * GLA_KEY), F32)
        aup = aup.at[0:GLA_RANK, 0:GLA_KEY].set(gla_a_up[l, 0])
        aup = aup.at[GLA_RANK:2 * GLA_RANK, GLA_KEY:2 * GLA_KEY].set(gla_a_up[l, 1]).astype(BF16)
        ab = gla_a_b[l].reshape(1, 2 * GLA_KEY)
        sgu_bias = jnp.repeat(sgu_b[l].T, D_MODEL // SGU_GROUPS, axis=1)
        mix_w = (
            wl[:, r_lo:r_hi].astype(BF16), wl[:, cv_lo:cv_hi].astype(BF16), wl[:, sg_lo:sg_hi].astype(BF16),
            wl[:, gt_lo:gt_hi].astype(BF16),
            row(gla_norm_g[l]), w_o_gla[l].astype(BF16),
            jnp.pad(conv_w[l], ((0, 1), (0, 0))), row(conv_b[l]), row(conv_ln_g[l]), row(conv_ln_b[l]),
            w_o_conv[l].astype(BF16),
            row(sgu_ln_g[l]), row(sgu_ln_b[l]), sgu_ws[l].astype(BF16), sgu_bias, w_o_sgu[l].astype(BF16),
            w_out[l].astype(BF16),
        )
        w1 = w_ffn_in[l].astype(BF16)
        w2 = w_ffn_out[l].astype(BF16)
        g1, g2 = row(norm1_g[l]), row(norm2_g[l])
        mod_lat = ada[l, 0:bsz]
        mod_ctx = jnp.broadcast_to(ada[l, bsz:bsz + 1], (bsz, 8, d))

        qc, kc, vc, lgc, obc, s_b = _gla_bwd(h_ctx, mod_ctx, g1, w_qkva, aup, ab, zero_state, n_tok=ctx_len)
        onc, s_f = _gla_fwd(qc, kc, vc, lgc, obc, zero_state, n_tok=ctx_len)
        if l < depth - 1:
            h_ctx = _mixer(h_ctx, onc, mod_ctx, g1, mix_w, n_tok=ctx_len, seg=ctx_len)
            h_ctx = _ffn(h_ctx, mod_ctx, g2, w1, w2, n_tok=ctx_len)

        q, k, v, lgf, ob, _ = _gla_bwd(h_lat, mod_lat, g1, w_qkva, aup, ab, s_b, n_tok=t_lat)
        on, _ = _gla_fwd(q, k, v, lgf, ob, s_f, n_tok=t_lat)
        h_lat = _mixer(h_lat, on, mod_lat, g1, mix_w, n_tok=t_mix, seg=GRID_W)
        h_lat = _ffn(h_lat, mod_lat, g2, w1, w2, n_tok=t_lat)
    return _final_norm(h_lat, row(final_g), n_tok=t_lat)
```

```python
import functools

import jax
import jax.numpy as jnp
from jax import lax
from jax.experimental import pallas as pl
from jax.experimental.pallas import tpu as pltpu

F32 = jnp.float32
BF16 = jnp.bfloat16

D_MODEL = 1024
GRID_W = 64
GLA_HEADS = 4
GLA_KEY = D_MODEL // 2
GLA_VAL = D_MODEL
GLA_DK = GLA_KEY // GLA_HEADS
GLA_DV = GLA_VAL // GLA_HEADS
GLA_RANK = 16
GLA_CHUNK = 64
GLA_GATE_NORM = 16.0
CONV_K = 31
CONV_PAD = CONV_K // 2
SGU_GROUPS = 8
SGU_CHUNK = 128
D_FF = ((8 * D_MODEL + 3 * 256 - 1) // (3 * 256)) * 256
EPS = 1e-6

LANES = 128
SUBLANES = 8
MXU_COLS = 256
QKV_COLS = 2 * GLA_KEY + GLA_VAL
ADN_PAD = LANES
CONV_HALO = 16
VMEM_LIMIT = 56 * 1024 * 1024


def _dot(a, b):
    return jnp.dot(a, b, preferred_element_type=F32)


def _dot_nt(a, b):
    return lax.dot_general(a, b, (((1,), (1,)), ((), ())), preferred_element_type=F32)


def _dot_tn(a, b):
    return lax.dot_general(a, b, (((0,), (0,)), ((), ())), preferred_element_type=F32)


def _rms_mod(h, g, shift, scale):
    y = h * lax.rsqrt(jnp.mean(h * h, axis=-1, keepdims=True) + EPS) * g
    return y * (1.0 + scale) + shift


def _layer_norm(x, g, b):
    xc = x - jnp.mean(x, axis=-1, keepdims=True)
    y = xc * lax.rsqrt(jnp.mean(xc * xc, axis=-1, keepdims=True) + EPS)
    return y * g + b


def _sigmoid(x):
    return 0.5 * jnp.tanh(0.5 * x) + 0.5


def _silu(x):
    return x * _sigmoid(x)


def _log_sigmoid(z):
    return jnp.minimum(z, 0.0) - jnp.log1p(jnp.exp(-jnp.abs(z)))


def _ada_kernel(c_ref, w_ref, b_ref, o_ref):
    o_ref[...] = _dot(_silu(c_ref[...]), w_ref[...]) + b_ref[...]


def _ada(cond, w_ada, b_ada):
    depth, d, n = w_ada.shape
    bn = n // 4
    return pl.pallas_call(
        _ada_kernel,
        grid=(depth, n // bn),
        in_specs=[
            pl.BlockSpec((8, d), lambda l, j: (0, 0)),
            pl.BlockSpec((None, d, bn), lambda l, j: (l, 0, j)),
            pl.BlockSpec((None, 1, bn), lambda l, j: (l, 0, j)),
        ],
        out_specs=pl.BlockSpec((None, 8, bn), lambda l, j: (l, 0, j)),
        out_shape=jax.ShapeDtypeStruct((depth, 8, n), F32),
        compiler_params=pltpu.CompilerParams(vmem_limit_bytes=VMEM_LIMIT),
        name="ada",
    )(cond, w_ada, b_ada.reshape(depth, 1, n))


def _gla_tile(q, k, v, lg, s_ref, o_ref, *, reverse, n_tok):
    c = GLA_CHUNK
    n_chunks = n_tok // c
    row = lax.broadcasted_iota(jnp.int32, (c, c), 0)
    col = lax.broadcasted_iota(jnp.int32, (c, c), 1)
    if reverse:
        tri = (col >= row).astype(BF16)
        keep = col > row
    else:
        tri = (col <= row).astype(BF16)
        keep = col <= row
    order = range(n_chunks - 1, -1, -1) if reverse else range(n_chunks)
    for ci in order:
        rows = slice(ci * c, (ci + 1) * c)
        lgc = lg[rows]
        lg_hi = lgc.astype(BF16)
        lg_lo = (lgc - lg_hi.astype(F32)).astype(BF16)
        b = _dot(tri, lg_hi) + _dot(tri, lg_lo)
        b_last = b[0:1] if reverse else b[c - 1:c]
        kc = k[rows]
        qe = (q[rows] * jnp.exp(b)).astype(BF16)
        km = (kc * jnp.exp(-b)).astype(BF16)
        kd = (kc * jnp.exp(b_last - b)).astype(BF16)
        dec = jnp.exp(b_last)
        for hh in range(GLA_HEADS):
            ksl = slice(hh * GLA_DK, (hh + 1) * GLA_DK)
            vsl = slice(hh * GLA_DV, (hh + 1) * GLA_DV)
            vh = v[rows, vsl]
            a = jnp.where(keep, _dot_nt(qe[:, ksl], km[:, ksl]), 0.0).astype(BF16)
            st = s_ref[hh]
            o_ref[rows, vsl] = _dot(a, vh) + _dot_nt(qe[:, ksl], st.astype(BF16))
            s_ref[hh] = st * dec[:, ksl] + _dot_tn(vh, kd[:, ksl])


def _gla_bwd_kernel(h_ref, mod_ref, g_ref, w_ref, aup_ref, ab_ref, s0_ref,
                    q_ref, k_ref, v_ref, lgf_ref, ob_ref, sfin_ref, s_scr, *, n_tok):
    t = pl.program_id(1)

    @pl.when(t == 0)
    def _():
        s_scr[...] = s0_ref[...]

    hm = _rms_mod(h_ref[...], g_ref[...], mod_ref[0:1, :], mod_ref[1:2, :]).astype(BF16)
    p = _dot(hm, w_ref[...])
    q = p[:, 0:GLA_KEY] * (GLA_DK ** -0.5)
    k = p[:, GLA_KEY:2 * GLA_KEY]
    v = p[:, 2 * GLA_KEY:QKV_COLS].astype(BF16)
    a_dn = p[:, QKV_COLS:QKV_COLS + ADN_PAD].astype(BF16)
    z = _dot(a_dn, aup_ref[...]) + ab_ref[...]
    lg = _log_sigmoid(z) * (1.0 / GLA_GATE_NORM)
    q_ref[...] = q.astype(BF16)
    k_ref[...] = k.astype(BF16)
    v_ref[...] = v
    lgf_ref[...] = lg[:, 0:GLA_KEY]
    _gla_tile(q, k, v, lg[:, GLA_KEY:2 * GLA_KEY], s_scr, ob_ref, reverse=True, n_tok=n_tok)

    @pl.when(t == pl.num_programs(1) - 1)
    def _():
        sfin_ref[...] = s_scr[...]


def _gla_fwd_kernel(q_ref, k_ref, v_ref, lgf_ref, ob_ref, s0_ref, on_ref, sfin_ref, s_scr, o_scr, *, n_tok):
    t = pl.program_id(1)

    @pl.when(t == 0)
    def _():
        s_scr[...] = s0_ref[...]

    _gla_tile(q_ref[...].astype(F32), k_ref[...].astype(F32), v_ref[...], lgf_ref[...], s_scr, o_scr,
              reverse=False, n_tok=n_tok)
    for hh in range(GLA_HEADS):
        vsl = slice(hh * GLA_DV, (hh + 1) * GLA_DV)
        o = o_scr[:, vsl] + ob_ref[:, vsl]
        on_ref[:, vsl] = o * lax.rsqrt(jnp.mean(o * o, axis=-1, keepdims=True) + EPS)

    @pl.when(t == pl.num_programs(1) - 1)
    def _():
        sfin_ref[...] = s_scr[...]


def _const_spec(shape):
    nd = len(shape)
    return pl.BlockSpec(shape, lambda b, t: (0,) * nd, pipeline_mode=pl.Buffered(1))


def _state_spec():
    return pl.BlockSpec((None, GLA_HEADS, GLA_DV, GLA_DK), lambda b, t: (b, 0, 0, 0))


def _gla_bwd(h, mod, g, w_qkva, aup, ab, s0, *, n_tok):
    bsz, seq, d = h.shape
    n_t = seq // n_tok
    rev = lambda b, t: (b, n_t - 1 - t, 0)
    tok = lambda w: pl.BlockSpec((None, n_tok, w), rev)
    return pl.pallas_call(
        functools.partial(_gla_bwd_kernel, n_tok=n_tok),
        grid=(bsz, n_t),
        in_specs=[
            tok(d),
            pl.BlockSpec((None, 8, d), lambda b, t: (b, 0, 0)),
            _const_spec((1, d)),
            _const_spec(w_qkva.shape),
            _const_spec(aup.shape),
            _const_spec(ab.shape),
            _state_spec(),
        ],
        out_specs=[tok(GLA_KEY), tok(GLA_KEY), tok(GLA_VAL), tok(GLA_KEY), tok(GLA_VAL), _state_spec()],
        out_shape=[
            jax.ShapeDtypeStruct((bsz, seq, GLA_KEY), BF16),
            jax.ShapeDtypeStruct((bsz, seq, GLA_KEY), BF16),
            jax.ShapeDtypeStruct((bsz, seq, GLA_VAL), BF16),
            jax.ShapeDtypeStruct((bsz, seq, GLA_KEY), F32),
            jax.ShapeDtypeStruct((bsz, seq, GLA_VAL), F32),
            jax.ShapeDtypeStruct((bsz, GLA_HEADS, GLA_DV, GLA_DK), F32),
        ],
        scratch_shapes=[pltpu.VMEM((GLA_HEADS, GLA_DV, GLA_DK), F32)],
        compiler_params=pltpu.CompilerParams(
            dimension_semantics=("arbitrary", "arbitrary"), vmem_limit_bytes=VMEM_LIMIT),
        name="gla_bwd",
    )(h, mod, g, w_qkva, aup, ab, s0)


def _gla_fwd(q, k, v, lgf, ob, s0, *, n_tok):
    bsz, seq, _ = v.shape
    n_t = seq // n_tok
    tok = lambda w: pl.BlockSpec((None, n_tok, w), lambda b, t: (b, t, 0))
    return pl.pallas_call(
        functools.partial(_gla_fwd_kernel, n_tok=n_tok),
        grid=(bsz, n_t),
        in_specs=[tok(GLA_KEY), tok(GLA_KEY), tok(GLA_VAL), tok(GLA_KEY), tok(GLA_VAL), _state_spec()],
        out_specs=[tok(GLA_VAL), _state_spec()],
        out_shape=[
            jax.ShapeDtypeStruct((bsz, seq, GLA_VAL), F32),
            jax.ShapeDtypeStruct((bsz, GLA_HEADS, GLA_DV, GLA_DK), F32),
        ],
        scratch_shapes=[pltpu.VMEM((GLA_HEADS, GLA_DV, GLA_DK), F32), pltpu.VMEM((n_tok, GLA_VAL), F32)],
        compiler_params=pltpu.CompilerParams(
            dimension_semantics=("arbitrary", "arbitrary"), vmem_limit_bytes=VMEM_LIMIT),
        name="gla_fwd",
    )(q, k, v, lgf, ob, s0)


def _interleave(primary, secondary):
    done = 0
    for i, job in enumerate(primary):
        job()
        while done < len(secondary) and (done + 1) * len(primary) <= (i + 1) * len(secondary):
            secondary[done]()
            done += 1
    for job in secondary[done:]:
        job()


def _mixer_kernel(h_ref, on_ref, mod_ref, g_ref, w_r_ref, w_conv_ref, w_sgu_ref, w_gate_ref,
                  gla_g_ref, w_o_gla_ref, cw_ref, cb_ref, cln_g_ref, cln_b_ref, w_o_conv_ref,
                  sln_g_ref, sln_b_ref, ws_ref, sb_ref, w_o_sgu_ref, w_out_ref,
                  out_ref, cpad, cout, sp_scr, r_scr, sg_scr, gt_scr, *, n_tok, seg):
    h = h_ref[...]
    hm = _rms_mod(h, g_ref[...], mod_ref[0:1, :], mod_ref[1:2, :]).astype(BF16)

    cin = _dot(hm, w_conv_ref[...])
    hc = cin[:, 0:D_MODEL] * _sigmoid(cin[:, D_MODEL:2 * D_MODEL])
    n_seg = n_tok // seg
    zeros = jnp.zeros((CONV_HALO, D_MODEL), F32)
    for s in range(n_seg):
        cpad[s, 0:CONV_HALO, :] = zeros
        cpad[s, CONV_HALO:CONV_HALO + seg, :] = hc[s * seg:(s + 1) * seg]
        cpad[s, CONV_HALO + seg:2 * CONV_HALO + seg, :] = zeros
    first = CONV_HALO - CONV_PAD
    m = seg + SUBLANES

    def conv_job(s, l0):
        def job():
            p = cpad[s, :, l0:l0 + LANES]
            acc = None
            for b in range(SUBLANES):
                yb = None
                for a in range((CONV_K + first + SUBLANES - 1) // SUBLANES):
                    j = SUBLANES * a + b - first
                    if 0 <= j < CONV_K:
                        term = cw_ref[j:j + 1, l0:l0 + LANES] * p[SUBLANES * a:SUBLANES * a + m]
                        yb = term if yb is None else yb + term
                shifted = yb[b:b + seg]
                acc = shifted if acc is None else acc + shifted
            cout[s * seg:(s + 1) * seg, l0:l0 + LANES] = acc
        return job

    def proj_job(w_ref, dst, c0):
        def job():
            dst[:, c0:c0 + MXU_COLS] = _dot(hm, w_ref[:, c0:c0 + MXU_COLS])
        return job

    conv_jobs = [conv_job(s, l0) for s in range(n_seg) for l0 in range(0, D_MODEL, LANES)]
    proj_jobs = [proj_job(w_ref, dst, c0)
                 for w_ref, dst in ((w_r_ref, r_scr), (w_sgu_ref, sg_scr), (w_gate_ref, gt_scr))
                 for c0 in range(0, dst.shape[1], MXU_COLS)]
    _interleave(conv_jobs, proj_jobs)

    y_a = _dot((on_ref[...] * gla_g_ref[...] * _silu(r_scr[...])).astype(BF16), w_o_gla_ref[...])

    su = jax.nn.gelu(sg_scr[:, 0:D_MODEL])
    sv = _layer_norm(jax.nn.gelu(sg_scr[:, D_MODEL:2 * D_MODEL]), sln_g_ref[...], sln_b_ref[...]).astype(BF16)
    gc = D_MODEL // SGU_GROUPS
    for n in range(n_tok // SGU_CHUNK):
        rows = slice(n * SGU_CHUNK, (n + 1) * SGU_CHUNK)
        for gi in range(SGU_GROUPS):
            lsl = slice(gi * gc, (gi + 1) * gc)
            sp_scr[rows, lsl] = _dot(ws_ref[gi], sv[rows, lsl]) + sb_ref[:, lsl]

    conv = cout[...] + cb_ref[...]
    y_b = _dot(_silu(_layer_norm(conv, cln_g_ref[...], cln_b_ref[...])).astype(BF16), w_o_conv_ref[...])

    y_c = _dot((su * sp_scr[...]).astype(BF16), w_o_sgu_ref[...])

    y = (_sigmoid(gt_scr[:, 0:D_MODEL]) * y_a + _sigmoid(gt_scr[:, D_MODEL:2 * D_MODEL]) * y_b
         + _sigmoid(gt_scr[:, 2 * D_MODEL:3 * D_MODEL]) * y_c)
    out_ref[...] = h + mod_ref[2:3, :] * _dot(y.astype(BF16), w_out_ref[...])


def _mixer(h, on, mod, g, wts, *, n_tok, seg):
    bsz, seq, d = h.shape
    n_t = seq // n_tok
    tok = pl.BlockSpec((None, n_tok, d), lambda b, t: (b, t, 0))
    return pl.pallas_call(
        functools.partial(_mixer_kernel, n_tok=n_tok, seg=seg),
        grid=(bsz, n_t),
        in_specs=[tok, tok, pl.BlockSpec((None, 8, d), lambda b, t: (b, 0, 0)), _const_spec((1, d))]
        + [_const_spec(w.shape) for w in wts],
        out_specs=tok,
        out_shape=jax.ShapeDtypeStruct((bsz, seq, d), F32),
        scratch_shapes=[
            pltpu.VMEM((n_tok // seg, seg + 2 * CONV_HALO, d), F32),
            pltpu.VMEM((n_tok, d), F32),
            pltpu.VMEM((n_tok, d), F32),
            pltpu.VMEM((n_tok, d), F32),
            pltpu.VMEM((n_tok, 2 * d), F32),
            pltpu.VMEM((n_tok, 3 * d), F32),
        ],
        compiler_params=pltpu.CompilerParams(
            dimension_semantics=("parallel", "parallel"), vmem_limit_bytes=VMEM_LIMIT),
        name="mixer",
    )(h, on, mod, g, *wts)


def _ffn_kernel(h_ref, mod_ref, g_ref, w1_ref, w2_ref, out_ref):
    h = h_ref[...]
    hm = _rms_mod(h, g_ref[...], mod_ref[3:4, :], mod_ref[4:5, :]).astype(BF16)
    gu = _dot(hm, w1_ref[...])
    act = (_silu(gu[:, 0:D_FF]) * gu[:, D_FF:2 * D_FF]).astype(BF16)
    out_ref[...] = h + mod_ref[5:6, :] * _dot(act, w2_ref[...])


def _ffn(h, mod, g, w1, w2, *, n_tok):
    bsz, seq, d = h.shape
    tok = pl.BlockSpec((None, n_tok, d), lambda b, t: (b, t, 0))
    return pl.pallas_call(
        functools.partial(_ffn_kernel),
        grid=(bsz, seq // n_tok),
        in_specs=[tok, pl.BlockSpec((None, 8, d), lambda b, t: (b, 0, 0)), _const_spec((1, d)),
                  _const_spec(w1.shape), _const_spec(w2.shape)],
        out_specs=tok,
        out_shape=jax.ShapeDtypeStruct((bsz, seq, d), F32),
        compiler_params=pltpu.CompilerParams(
            dimension_semantics=("parallel", "parallel"), vmem_limit_bytes=VMEM_LIMIT),
        name="ffn",
    )(h, mod, g, w1, w2)


def _final_norm_kernel(h_ref, g_ref, o_ref):
    h = h_ref[...]
    o_ref[...] = h * lax.rsqrt(jnp.mean(h * h, axis=-1, keepdims=True) + EPS) * g_ref[...]


def _final_norm(h, g, *, n_tok):
    bsz, seq, d = h.shape
    tok = pl.BlockSpec((None, n_tok, d), lambda b, t: (b, t, 0))
    return pl.pallas_call(
        _final_norm_kernel,
        grid=(bsz, seq // n_tok),
        in_specs=[tok, _const_spec((1, d))],
        out_specs=tok,
        out_shape=jax.ShapeDtypeStruct((bsz, seq, d), F32),
        compiler_params=pltpu.CompilerParams(dimension_semantics=("parallel", "parallel")),
        name="final_norm",
    )(h, g)


def _pick_tile(seq, want):
    return min(seq, want)


def kernel(x, c, ctx, c_ctx, w_ada, b_ada, norm1_g, norm2_g, w_in, gla_a_up, gla_a_b, gla_norm_g, w_o_gla,
           conv_w, conv_b, conv_ln_g, conv_ln_b, w_o_conv, sgu_ln_g, sgu_ln_b, sgu_ws, sgu_b, w_o_sgu, w_out,
           w_ffn_in, w_ffn_out, final_g):
    bsz, seq, d = x.shape
    depth = w_ada.shape[0]
    ctx_len = ctx.shape[1]

    cond = jnp.zeros((8, d), F32).at[0:bsz].set(c).at[bsz].set(c_ctx)
    ada = _ada(cond, w_ada, b_ada).reshape(depth, 8, 6, d)
    ada = jnp.pad(ada, ((0, 0), (0, 0), (0, 2), (0, 0)))

    row = lambda a: a.reshape(1, -1)
    c0 = 0
    offs = []
    for wdt in (GLA_KEY, GLA_KEY, GLA_VAL, 2 * GLA_RANK, GLA_VAL, 2 * D_MODEL, 2 * D_MODEL, 3 * D_MODEL):
        offs.append((c0, c0 + wdt))
        c0 += wdt
    (_, _), (_, _), (_, v_end), (a_lo, a_hi), (r_lo, r_hi), (cv_lo, cv_hi), (sg_lo, sg_hi), (gt_lo, gt_hi) = offs

    h_lat, h_ctx = x, ctx
    zero_state = jnp.zeros((bsz, GLA_HEADS, GLA_DV, GLA_DK), F32)
    t_lat = _pick_tile(seq, 512)
    t_mix = _pick_tile(seq, 256)
    for l in range(depth):
        wl = w_in[l]
        w_qkva = jnp.pad(wl[:, 0:a_hi], ((0, 0), (0, ADN_PAD - (a_hi - a_lo)))).astype(BF16)
        aup = jnp.zeros((ADN_PAD, 2 * GLA_KEY), F32)
        aup = aup.at[0:GLA_RANK, 0:GLA_KEY].set(gla_a_up[l, 0])
        aup = aup.at[GLA_RANK:2 * GLA_RANK, GLA_KEY:2 * GLA_KEY].set(gla_a_up[l, 1]).astype(BF16)
        ab = gla_a_b[l].reshape(1, 2 * GLA_KEY)
        sgu_bias = jnp.repeat(sgu_b[l].T, D_MODEL // SGU_GROUPS, axis=1)
        mix_w = (
            wl[:, r_lo:r_hi].astype(BF16), wl[:, cv_lo:cv_hi].astype(BF16), wl[:, sg_lo:sg_hi].astype(BF16),
            wl[:, gt_lo:gt_hi].astype(BF16),
            row(gla_norm_g[l]), w_o_gla[l].astype(BF16),
            jnp.pad(conv_w[l], ((0, 1), (0, 0))), row(conv_b[l]), row(conv_ln_g[l]), row(conv_ln_b[l]),
            w_o_conv[l].astype(BF16),
            row(sgu_ln_g[l]), row(sgu_ln_b[l]), sgu_ws[l].astype(BF16), sgu_bias, w_o_sgu[l].astype(BF16),
            w_out[l].astype(BF16),
        )
        w1 = w_ffn_in[l].astype(BF16)
        w2 = w_ffn_out[l].astype(BF16)
        g1, g2 = row(norm1_g[l]), row(norm2_g[l])
        mod_lat = ada[l, 0:bsz]
        mod_ctx = jnp.broadcast_to(ada[l, bsz:bsz + 1], (bsz, 8, d))

        qc, kc, vc, lgc, obc, s_b = _gla_bwd(h_ctx, mod_ctx, g1, w_qkva, aup, ab, zero_state, n_tok=ctx_len)
        onc, s_f = _gla_fwd(qc, kc, vc, lgc, obc, zero_state, n_tok=ctx_len)
        if l < depth - 1:
            h_ctx = _mixer(h_ctx, onc, mod_ctx, g1, mix_w, n_tok=ctx_len, seg=ctx_len)
            h_ctx = _ffn(h_ctx, mod_ctx, g2, w1, w2, n_tok=ctx_len)

        q, k, v, lgf, ob, _ = _gla_bwd(h_lat, mod_lat, g1, w_qkva, aup, ab, s_b, n_tok=t_lat)
        on, _ = _gla_fwd(q, k, v, lgf, ob, s_f, n_tok=t_lat)
        h_lat = _mixer(h_lat, on, mod_lat, g1, mix_w, n_tok=t_mix, seg=GRID_W)
        h_lat = _ffn(h_lat, mod_lat, g2, w1, w2, n_tok=t_lat)
    return _final_norm(h_lat, row(final_g), n_tok=t_lat)
```

```python
import functools

import jax
import jax.numpy as jnp
from jax import lax
from jax.experimental import pallas as pl
from jax.experimental.pallas import tpu as pltpu

F32 = jnp.float32
BF16 = jnp.bfloat16

D_MODEL = 1024
GRID_W = 64
GLA_HEADS = 4
GLA_KEY = D_MODEL // 2
GLA_VAL = D_MODEL
GLA_DK = GLA_KEY // GLA_HEADS
GLA_DV = GLA_VAL // GLA_HEADS
GLA_RANK = 16
GLA_CHUNK = 64
GLA_GATE_NORM = 16.0
CONV_K = 31
CONV_PAD = CONV_K // 2
SGU_GROUPS = 8
SGU_CHUNK = 128
D_FF = ((8 * D_MODEL + 3 * 256 - 1) // (3 * 256)) * 256
EPS = 1e-6

LANES = 128
SUBLANES = 8
MXU_COLS = 256
QKV_COLS = 2 * GLA_KEY + GLA_VAL
ADN_PAD = LANES
CONV_HALO = 16
VMEM_LIMIT = 56 * 1024 * 1024


def _dot(a, b):
    return jnp.dot(a, b, preferred_element_type=F32)


def _dot_nt(a, b):
    return lax.dot_general(a, b, (((1,), (1,)), ((), ())), preferred_element_type=F32)


def _dot_tn(a, b):
    return lax.dot_general(a, b, (((0,), (0,)), ((), ())), preferred_element_type=F32)


def _rms_mod(h, g, shift, scale):
    y = h * lax.rsqrt(jnp.mean(h * h, axis=-1, keepdims=True) + EPS) * g
    return y * (1.0 + scale) + shift


def _layer_norm(x, g, b):
    xc = x - jnp.mean(x, axis=-1, keepdims=True)
    y = xc * lax.rsqrt(jnp.mean(xc * xc, axis=-1, keepdims=True) + EPS)
    return y * g + b


def _sigmoid(x):
    return 0.5 * jnp.tanh(0.5 * x) + 0.5


def _silu(x):
    return x * _sigmoid(x)


def _log_sigmoid(z):
    return jnp.minimum(z, 0.0) - jnp.log1p(jnp.exp(-jnp.abs(z)))


def _ada_kernel(c_ref, w_ref, b_ref, o_ref):
    o_ref[...] = _dot(_silu(c_ref[...]), w_ref[...]) + b_ref[...]


def _ada(cond, w_ada, b_ada):
    depth, d, n = w_ada.shape
    bn = n // 4
    return pl.pallas_call(
        _ada_kernel,
        grid=(depth, n // bn),
        in_specs=[
            pl.BlockSpec((8, d), lambda l, j: (0, 0)),
            pl.BlockSpec((None, d, bn), lambda l, j: (l, 0, j)),
            pl.BlockSpec((None, 1, bn), lambda l, j: (l, 0, j)),
        ],
        out_specs=pl.BlockSpec((None, 8, bn), lambda l, j: (l, 0, j)),
        out_shape=jax.ShapeDtypeStruct((depth, 8, n), F32),
        compiler_params=pltpu.CompilerParams(vmem_limit_bytes=VMEM_LIMIT),
        name="ada",
    )(cond, w_ada, b_ada.reshape(depth, 1, n))


def _gla_tile(q, k, v, lg, s_ref, o_ref, *, reverse, n_tok):
    c = GLA_CHUNK
    n_chunks = n_tok // c
    row = lax.broadcasted_iota(jnp.int32, (c, c), 0)
    col = lax.broadcasted_iota(jnp.int32, (c, c), 1)
    if reverse:
        tri = (col >= row).astype(BF16)
        keep = col > row
    else:
        tri = (col <= row).astype(BF16)
        keep = col <= row
    order = range(n_chunks - 1, -1, -1) if reverse else range(n_chunks)
    chunk_rows = [slice(ci * c, (ci + 1) * c) for ci in range(n_chunks)]
    ksls = [slice(hh * GLA_DK, (hh + 1) * GLA_DK) for hh in range(GLA_HEADS)]
    vsls = [slice(hh * GLA_DV, (hh + 1) * GLA_DV) for hh in range(GLA_HEADS)]

    lg_hi = lg.astype(BF16)
    lg_lo = (lg - lg_hi.astype(F32)).astype(BF16)
    cum = _dot(tri, jnp.concatenate([lg_hi[r] for r in chunk_rows] + [lg_lo[r] for r in chunk_rows], axis=1))
    qe, km, kd, dec = [], [], [], []
    for ci, rows in enumerate(chunk_rows):
        b = (cum[:, ci * GLA_KEY:(ci + 1) * GLA_KEY]
             + cum[:, (n_chunks + ci) * GLA_KEY:(n_chunks + ci + 1) * GLA_KEY])
        b_last = b[0:1] if reverse else b[c - 1:c]
        kc = k[rows]
        qe.append((q[rows] * jnp.exp(b)).astype(BF16))
        km.append((kc * jnp.exp(-b)).astype(BF16))
        kd.append((kc * jnp.exp(b_last - b)).astype(BF16))
        dec.append(jnp.exp(b_last))
    heads = list(zip(ksls, vsls))
    scores = [[_dot_nt(qe[ci][:, ksl], km[ci][:, ksl]) for ksl, _ in heads] for ci in range(n_chunks)]
    upd = [[_dot_tn(v[rows, vsl], kd[ci][:, ksl]) for ksl, vsl in heads]
           for ci, rows in enumerate(chunk_rows)]
    for ci, rows in enumerate(chunk_rows):
        for hh, (_, vsl) in enumerate(heads):
            a = jnp.where(keep, scores[ci][hh], 0.0).astype(BF16)
            o_ref[rows, vsl] = _dot(a, v[rows, vsl])

    st = [s_ref[hh] for hh in range(GLA_HEADS)]
    s_in = [None] * n_chunks
    for ci in order:
        s_in[ci] = [s.astype(BF16) for s in st]
        st = [st[hh] * dec[ci][:, ksls[hh]] + upd[ci][hh] for hh in range(GLA_HEADS)]
    for hh in range(GLA_HEADS):
        s_ref[hh] = st[hh]

    for ci, rows in enumerate(chunk_rows):
        for hh, (ksl, vsl) in enumerate(zip(ksls, vsls)):
            o_ref[rows, vsl] += _dot_nt(qe[ci][:, ksl], s_in[ci][hh])


def _gla_bwd_kernel(h_ref, mod_ref, g_ref, w_ref, aup_ref, ab_ref, s0_ref,
                    q_ref, k_ref, v_ref, lgf_ref, ob_ref, sfin_ref, s_scr, *, n_tok):
    t = pl.program_id(1)

    @pl.when(t == 0)
    def _():
        s_scr[...] = s0_ref[...]

    hm = _rms_mod(h_ref[...], g_ref[...], mod_ref[0:1, :], mod_ref[1:2, :]).astype(BF16)
    a_dn = _dot(hm, w_ref[:, QKV_COLS:QKV_COLS + ADN_PAD]).astype(BF16)
    z = _dot(a_dn, aup_ref[...]) + ab_ref[...]
    lg = _log_sigmoid(z) * (1.0 / GLA_GATE_NORM)
    p = _dot(hm, w_ref[:, 0:QKV_COLS])
    q = p[:, 0:GLA_KEY] * (GLA_DK ** -0.5)
    k = p[:, GLA_KEY:2 * GLA_KEY]
    v = p[:, 2 * GLA_KEY:QKV_COLS].astype(BF16)
    q_ref[...] = q.astype(BF16)
    k_ref[...] = k.astype(BF16)
    v_ref[...] = v
    lgf_ref[...] = lg[:, 0:GLA_KEY]
    _gla_tile(q, k, v, lg[:, GLA_KEY:2 * GLA_KEY], s_scr, ob_ref, reverse=True, n_tok=n_tok)

    @pl.when(t == pl.num_programs(1) - 1)
    def _():
        sfin_ref[...] = s_scr[...]


def _gla_fwd_kernel(q_ref, k_ref, v_ref, lgf_ref, ob_ref, s0_ref, on_ref, sfin_ref, s_scr, o_scr, *, n_tok):
    t = pl.program_id(1)

    @pl.when(t == 0)
    def _():
        s_scr[...] = s0_ref[...]

    _gla_tile(q_ref[...].astype(F32), k_ref[...].astype(F32), v_ref[...], lgf_ref[...], s_scr, o_scr,
              reverse=False, n_tok=n_tok)
    for hh in range(GLA_HEADS):
        vsl = slice(hh * GLA_DV, (hh + 1) * GLA_DV)
        o = o_scr[:, vsl] + ob_ref[:, vsl]
        on_ref[:, vsl] = o * lax.rsqrt(jnp.mean(o * o, axis=-1, keepdims=True) + EPS)

    @pl.when(t == pl.num_programs(1) - 1)
    def _():
        sfin_ref[...] = s_scr[...]


def _const_spec(shape):
    nd = len(shape)
    return pl.BlockSpec(shape, lambda b, t: (0,) * nd, pipeline_mode=pl.Buffered(1))


def _state_spec():
    return pl.BlockSpec((None, GLA_HEADS, GLA_DV, GLA_DK), lambda b, t: (b, 0, 0, 0))


def _gla_bwd(h, mod, g, w_qkva, aup, ab, s0, *, n_tok):
    bsz, seq, d = h.shape
    n_t = seq // n_tok
    rev = lambda b, t: (b, n_t - 1 - t, 0)
    tok = lambda w: pl.BlockSpec((None, n_tok, w), rev)
    return pl.pallas_call(
        functools.partial(_gla_bwd_kernel, n_tok=n_tok),
        grid=(bsz, n_t),
        in_specs=[
            tok(d),
            pl.BlockSpec((None, 8, d), lambda b, t: (b, 0, 0)),
            _const_spec((1, d)),
            _const_spec(w_qkva.shape),
            _const_spec(aup.shape),
            _const_spec(ab.shape),
            _state_spec(),
        ],
        out_specs=[tok(GLA_KEY), tok(GLA_KEY), tok(GLA_VAL), tok(GLA_KEY), tok(GLA_VAL), _state_spec()],
        out_shape=[
            jax.ShapeDtypeStruct((bsz, seq, GLA_KEY), BF16),
            jax.ShapeDtypeStruct((bsz, seq, GLA_KEY), BF16),
            jax.ShapeDtypeStruct((bsz, seq, GLA_VAL), BF16),
            jax.ShapeDtypeStruct((bsz, seq, GLA_KEY), F32),
            jax.ShapeDtypeStruct((bsz, seq, GLA_VAL), F32),
            jax.ShapeDtypeStruct((bsz, GLA_HEADS, GLA_DV, GLA_DK), F32),
        ],
        scratch_shapes=[pltpu.VMEM((GLA_HEADS, GLA_DV, GLA_DK), F32)],
        compiler_params=pltpu.CompilerParams(
            dimension_semantics=("arbitrary", "arbitrary"), vmem_limit_bytes=VMEM_LIMIT),
        name="gla_bwd",
    )(h, mod, g, w_qkva, aup, ab, s0)


def _gla_fwd(q, k, v, lgf, ob, s0, *, n_tok):
    bsz, seq, _ = v.shape
    n_t = seq // n_tok
    tok = lambda w: pl.BlockSpec((None, n_tok, w), lambda b, t: (b, t, 0))
    return pl.pallas_call(
        functools.partial(_gla_fwd_kernel, n_tok=n_tok),
        grid=(bsz, n_t),
        in_specs=[tok(GLA_KEY), tok(GLA_KEY), tok(GLA_VAL), tok(GLA_KEY), tok(GLA_VAL), _state_spec()],
        out_specs=[tok(GLA_VAL), _state_spec()],
        out_shape=[
            jax.ShapeDtypeStruct((bsz, seq, GLA_VAL), F32),
            jax.ShapeDtypeStruct((bsz, GLA_HEADS, GLA_DV, GLA_DK), F32),
        ],
        scratch_shapes=[pltpu.VMEM((GLA_HEADS, GLA_DV, GLA_DK), F32), pltpu.VMEM((n_tok, GLA_VAL), F32)],
        compiler_params=pltpu.CompilerParams(
            dimension_semantics=("arbitrary", "arbitrary"), vmem_limit_bytes=VMEM_LIMIT),
        name="gla_fwd",
    )(q, k, v, lgf, ob, s0)


class _Jobs:
    def __init__(self):
        self._jobs = []

    def add(self, unit, cost, fn, deps=()):
        self._jobs.append((unit, cost, fn, tuple(deps)))
        return len(self._jobs) - 1

    def emit(self):
        free = {"m": 0.0, "v": 0.0}
        start, finish = [], []
        for unit, cost, _, deps in self._jobs:
            t0 = max([free[unit]] + [finish[d] for d in deps])
            start.append(t0)
            finish.append(t0 + cost)
            free[unit] = t0 + cost
        for i in sorted(range(len(self._jobs)), key=lambda i: (start[i], i)):
            self._jobs[i][2]()


def _mixer_kernel(h_ref, on_ref, mod_ref, g_ref, w_r_ref, w_conv_ref, w_sgu_ref, w_gate_ref,
                  gla_g_ref, w_o_gla_ref, cw_ref, cb_ref, cln_g_ref, cln_b_ref, w_o_conv_ref,
                  sln_g_ref, sln_b_ref, ws_ref, sb_ref, w_o_sgu_ref, w_out_ref,
                  out_ref, cpad, cout, cin_scr, r_scr, sg_scr, gt_scr, sp_scr, ya_scr, yb_scr, yc_scr,
                  ain_scr, bin_scr, cin2_scr, yin_scr, *, n_tok, seg):
    d = D_MODEL
    n_seg = n_tok // seg
    n_col = d // MXU_COLS
    col = lambda c: slice(c * MXU_COLS, (c + 1) * MXU_COLS)
    mxu_cost = n_tok
    jobs = _Jobs()
    val = {}

    def f_hm():
        val["hm"] = _rms_mod(h_ref[...], g_ref[...], mod_ref[0:1, :], mod_ref[1:2, :]).astype(BF16)
    j_hm = jobs.add("v", 2.5 * n_tok, f_hm)

    def proj(w_ref, dst, c):
        def job():
            dst[:, col(c)] = _dot(val["hm"], w_ref[:, col(c)])
        return jobs.add("m", mxu_cost, job, (j_hm,))

    first = CONV_HALO - CONV_PAD
    m = seg + SUBLANES

    def glu(c):
        def job():
            hc = cin_scr[:, col(c)] * _sigmoid(cin_scr[:, d + c * MXU_COLS:d + (c + 1) * MXU_COLS])
            zeros = jnp.zeros((CONV_HALO, MXU_COLS), F32)
            for s in range(n_seg):
                cpad[s, 0:CONV_HALO, col(c)] = zeros
                cpad[s, CONV_HALO:CONV_HALO + seg, col(c)] = hc[s * seg:(s + 1) * seg]
                cpad[s, CONV_HALO + seg:2 * CONV_HALO + seg, col(c)] = zeros
        return job

    def conv(s, l0):
        def job():
            p = cpad[s, :, l0:l0 + LANES]
            acc = None
            for b in range(SUBLANES):
                yb = None
                for a in range((CONV_K + first + SUBLANES - 1) // SUBLANES):
                    j = SUBLANES * a + b - first
                    if 0 <= j < CONV_K:
                        term = cw_ref[j:j + 1, l0:l0 + LANES] * p[SUBLANES * a:SUBLANES * a + m]
                        yb = term if yb is None else yb + term
                shifted = yb[b:b + seg]
                acc = shifted if acc is None else acc + shifted
            cout[s * seg:(s + 1) * seg, l0:l0 + LANES] = acc
        return job

    j_conv = []
    for c in range(n_col):
        ja = proj(w_conv_ref, cin_scr, c)
        jb = proj(w_conv_ref, cin_scr, n_col + c)
        jg = jobs.add("v", 0.4 * n_tok, glu(c), (ja, jb))
        for l0 in range(c * MXU_COLS, (c + 1) * MXU_COLS, LANES):
            for s in range(n_seg):
                j_conv.append(jobs.add("v", 22 * (seg // SUBLANES), conv(s, l0), (jg,)))

    j_r = [proj(w_r_ref, r_scr, c) for c in range(n_col)]
    j_sg = [proj(w_sgu_ref, sg_scr, c) for c in range(2 * n_col)]
    j_gt = [proj(w_gate_ref, gt_scr, c) for c in range(3 * n_col)]

    def out_proj(src, w_ref, dst, deps):
        def chunk(c):
            def job():
                dst[:, col(c)] = _dot(src[...], w_ref[:, col(c)])
            return job
        return [jobs.add("m", mxu_cost, chunk(c), deps) for c in range(n_col)]

    def f_a():
        ain_scr[...] = (on_ref[...] * gla_g_ref[...] * _silu(r_scr[...])).astype(BF16)
    j_ya = out_proj(ain_scr, w_o_gla_ref, ya_scr, (jobs.add("v", 2 * n_tok, f_a, j_r),))

    gc = d // SGU_GROUPS
    j_cin = []
    for n in range(n_tok // SGU_CHUNK):
        rows = slice(n * SGU_CHUNK, (n + 1) * SGU_CHUNK)

        def f_sv(rows=rows):
            val["sv", rows.start] = _layer_norm(jax.nn.gelu(sg_scr[rows, d:2 * d]), sln_g_ref[...],
                                                sln_b_ref[...]).astype(BF16)

        def f_ws(rows=rows):
            sv = val["sv", rows.start]
            for gi in range(SGU_GROUPS):
                lsl = slice(gi * gc, (gi + 1) * gc)
                sp_scr[rows, lsl] = _dot(ws_ref[gi], sv[:, lsl]) + sb_ref[:, lsl]

        def f_c(rows=rows):
            cin2_scr[rows, :] = (jax.nn.gelu(sg_scr[rows, 0:d]) * sp_scr[rows, :]).astype(BF16)

        j_sv = jobs.add("v", 5 * SGU_CHUNK, f_sv, j_sg[n_col:])
        j_ws = jobs.add("m", 4 * SGU_CHUNK, f_ws, (j_sv,))
        j_cin.append(jobs.add("v", 3.5 * SGU_CHUNK, f_c, [j_ws] + j_sg[:n_col]))
    j_yc = out_proj(cin2_scr, w_o_sgu_ref, yc_scr, j_cin)

    j_bin = []
    for n in range(n_tok // SGU_CHUNK):
        rows = slice(n * SGU_CHUNK, (n + 1) * SGU_CHUNK)

        def f_b(rows=rows):
            bin_scr[rows, :] = _silu(_layer_norm(cout[rows, :] + cb_ref[...], cln_g_ref[...],
                                                 cln_b_ref[...])).astype(BF16)
        j_bin.append(jobs.add("v", 4 * SGU_CHUNK, f_b, j_conv))
    j_yb = out_proj(bin_scr, w_o_conv_ref, yb_scr, j_bin)

    j_mg = []
    for c in range(n_col):
        def f_m(c=c):
            y = (_sigmoid(gt_scr[:, col(c)]) * ya_scr[:, col(c)]
                 + _sigmoid(gt_scr[:, d + c * MXU_COLS:d + (c + 1) * MXU_COLS]) * yb_scr[:, col(c)]
                 + _sigmoid(gt_scr[:, 2 * d + c * MXU_COLS:2 * d + (c + 1) * MXU_COLS]) * yc_scr[:, col(c)])
            yin_scr[:, col(c)] = y.astype(BF16)
        j_mg.append(jobs.add("v", 1.2 * n_tok, f_m,
                             (j_gt[c], j_gt[n_col + c], j_gt[2 * n_col + c], j_ya[c], j_yb[c], j_yc[c])))
    for c in range(n_col):
        def f_o(c=c):
            out_ref[:, col(c)] = h_ref[:, col(c)] + mod_ref[2:3, col(c)] * _dot(yin_scr[...], w_out_ref[:, col(c)])
        jobs.add("m", mxu_cost, f_o, j_mg)

    jobs.emit()


def _mixer(h, on, mod, g, wts, *, n_tok, seg):
    bsz, seq, d = h.shape
    n_t = seq // n_tok
    tok = pl.BlockSpec((None, n_tok, d), lambda b, t: (b, t, 0))
    f32s = lambda w: pltpu.VMEM((n_tok, w), F32)
    bf16s = lambda w: pltpu.VMEM((n_tok, w), BF16)
    return pl.pallas_call(
        functools.partial(_mixer_kernel, n_tok=n_tok, seg=seg),
        grid=(bsz, n_t),
        in_specs=[tok, tok, pl.BlockSpec((None, 8, d), lambda b, t: (b, 0, 0)), _const_spec((1, d))]
        + [_const_spec(w.shape) for w in wts],
        out_specs=tok,
        out_shape=jax.ShapeDtypeStruct((bsz, seq, d), F32),
        scratch_shapes=[
            pltpu.VMEM((n_tok // seg, seg + 2 * CONV_HALO, d), F32),
            f32s(d), f32s(2 * d), f32s(d), f32s(2 * d), f32s(3 * d), f32s(d),
            f32s(d), f32s(d), f32s(d),
            bf16s(d), bf16s(d), bf16s(d), bf16s(d),
        ],
        compiler_params=pltpu.CompilerParams(
            dimension_semantics=("parallel", "parallel"), vmem_limit_bytes=VMEM_LIMIT),
        name="mixer",
    )(h, on, mod, g, *wts)


def _ffn_kernel(*refs, final):
    if final:
        h_ref, mod_ref, g_ref, w1_ref, w2_ref, fg_ref, out_ref = refs
    else:
        h_ref, mod_ref, g_ref, w1_ref, w2_ref, out_ref = refs
    h = h_ref[...]
    hm = _rms_mod(h, g_ref[...], mod_ref[3:4, :], mod_ref[4:5, :]).astype(BF16)
    gu = _dot(hm, w1_ref[...])
    act = (_silu(gu[:, 0:D_FF]) * gu[:, D_FF:2 * D_FF]).astype(BF16)
    out = h + mod_ref[5:6, :] * _dot(act, w2_ref[...])
    if final:
        out = out * lax.rsqrt(jnp.mean(out * out, axis=-1, keepdims=True) + EPS) * fg_ref[...]
    out_ref[...] = out


def _ffn(h, mod, g, w1, w2, final_g=None, *, n_tok):
    bsz, seq, d = h.shape
    tok = pl.BlockSpec((None, n_tok, d), lambda b, t: (b, t, 0))
    final = final_g is not None
    extra = (final_g,) if final else ()
    return pl.pallas_call(
        functools.partial(_ffn_kernel, final=final),
        grid=(bsz, seq // n_tok),
        in_specs=[tok, pl.BlockSpec((None, 8, d), lambda b, t: (b, 0, 0)), _const_spec((1, d)),
                  _const_spec(w1.shape), _const_spec(w2.shape)] + [_const_spec((1, d)) for _ in extra],
        out_specs=tok,
        out_shape=jax.ShapeDtypeStruct((bsz, seq, d), F32),
        compiler_params=pltpu.CompilerParams(
            dimension_semantics=("parallel", "parallel"), vmem_limit_bytes=VMEM_LIMIT),
        name="ffn",
    )(h, mod, g, w1, w2, *extra)


def _pick_tile(seq, want):
    return min(seq, want)


def kernel(x, c, ctx, c_ctx, w_ada, b_ada, norm1_g, norm2_g, w_in, gla_a_up, gla_a_b, gla_norm_g, w_o_gla,
           conv_w, conv_b, conv_ln_g, conv_ln_b, w_o_conv, sgu_ln_g, sgu_ln_b, sgu_ws, sgu_b, w_o_sgu, w_out,
           w_ffn_in, w_ffn_out, final_g):
    bsz, seq, d = x.shape
    depth = w_ada.shape[0]
    ctx_len = ctx.shape[1]

    cond = jnp.zeros((8, d), F32).at[0:bsz].set(c).at[bsz].set(c_ctx)
    ada = _ada(cond, w_ada, b_ada).reshape(depth, 8, 6, d)
    ada = jnp.pad(ada, ((0, 0), (0, 0), (0, 2), (0, 0)))

    row = lambda a: a.reshape(1, -1)
    c0 = 0
    offs = []
    for wdt in (GLA_KEY, GLA_KEY, GLA_VAL, 2 * GLA_RANK, GLA_VAL, 2 * D_MODEL, 2 * D_MODEL, 3 * D_MODEL):
        offs.append((c0, c0 + wdt))
        c0 += wdt
    (_, _), (_, _), (_, v_end), (a_lo, a_hi), (r_lo, r_hi), (cv_lo, cv_hi), (sg_lo, sg_hi), (gt_lo, gt_hi) = offs

    h_lat, h_ctx = x, ctx
    zero_state = jnp.zeros((bsz, GLA_HEADS, GLA_DV, GLA_DK), F32)
    t_lat = _pick_tile(seq, 512)
    t_mix = _pick_tile(seq, 256)
    for l in range(depth):
        wl = w_in[l]
        w_qkva = jnp.pad(wl[:, 0:a_hi], ((0, 0), (0, ADN_PAD - (a_hi - a_lo)))).astype(BF16)
        aup = jnp.zeros((ADN_PAD, 2 * GLA_KEY), F32)
        aup = aup.at[0:GLA_RANK, 0:GLA_KEY].set(gla_a_up[l, 0])
        aup = aup.at[GLA_RANK:2 * GLA_RANK, GLA_KEY:2 * GLA_KEY].set(gla_a_up[l, 1]).astype(BF16)
        ab = gla_a_b[l].reshape(1, 2 * GLA_KEY)
        sgu_bias = jnp.repeat(sgu_b[l].T, D_MODEL // SGU_GROUPS, axis=1)
        mix_w = (
            wl[:, r_lo:r_hi].astype(BF16), wl[:, cv_lo:cv_hi].astype(BF16), wl[:, sg_lo:sg_hi].astype(BF16),
            wl[:, gt_lo:gt_hi].astype(BF16),
            row(gla_norm_g[l]), w_o_gla[l].astype(BF16),
            jnp.pad(conv_w[l], ((0, 1), (0, 0))), row(conv_b[l]), row(conv_ln_g[l]), row(conv_ln_b[l]),
            w_o_conv[l].astype(BF16),
            row(sgu_ln_g[l]), row(sgu_ln_b[l]), sgu_ws[l].astype(BF16), sgu_bias, w_o_sgu[l].astype(BF16),
            w_out[l].astype(BF16),
        )
        w1 = w_ffn_in[l].astype(BF16)
        w2 = w_ffn_out[l].astype(BF16)
        g1, g2 = row(norm1_g[l]), row(norm2_g[l])
        mod_lat = ada[l, 0:bsz]
        mod_ctx = jnp.broadcast_to(ada[l, bsz:bsz + 1], (bsz, 8, d))

        qc, kc, vc, lgc, obc, s_b = _gla_bwd(h_ctx, mod_ctx, g1, w_qkva, aup, ab, zero_state, n_tok=ctx_len)
        onc, s_f = _gla_fwd(qc, kc, vc, lgc, obc, zero_state, n_tok=ctx_len)
        if l < depth - 1:
            h_ctx = _mixer(h_ctx, onc, mod_ctx, g1, mix_w, n_tok=ctx_len, seg=ctx_len)
            h_ctx = _ffn(h_ctx, mod_ctx, g2, w1, w2, n_tok=ctx_len)

        q, k, v, lgf, ob, _ = _gla_bwd(h_lat, mod_lat, g1, w_qkva, aup, ab, s_b, n_tok=t_lat)
        on, _ = _gla_fwd(q, k, v, lgf, ob, s_f, n_tok=t_lat)
        h_lat = _mixer(h_lat, on, mod_lat, g1, mix_w, n_tok=t_mix, seg=GRID_W)
        h_lat = _ffn(h_lat, mod_lat, g2, w1, w2, row(final_g) if l == depth - 1 else None, n_tok=t_lat)
    return h_lat
```

```python
import functools

import jax
import jax.numpy as jnp
from jax import lax
from jax.experimental import pallas as pl
from jax.experimental.pallas import tpu as pltpu

F32 = jnp.float32
BF16 = jnp.bfloat16

D_MODEL = 1024
GRID_W = 64
GLA_HEADS = 4
GLA_KEY = D_MODEL // 2
GLA_VAL = D_MODEL
GLA_DK = GLA_KEY // GLA_HEADS
GLA_DV = GLA_VAL // GLA_HEADS
GLA_RANK = 16
GLA_CHUNK = 64
GLA_GATE_NORM = 16.0
CONV_K = 31
CONV_PAD = CONV_K // 2
SGU_GROUPS = 8
SGU_CHUNK = 128
D_FF = ((8 * D_MODEL + 3 * 256 - 1) // (3 * 256)) * 256
EPS = 1e-6

LANES = 128
SUBLANES = 8
MXU_COLS = 256
QKV_COLS = 2 * GLA_KEY + GLA_VAL
ADN_PAD = LANES
CONV_HALO = 16
VMEM_LIMIT = 56 * 1024 * 1024


def _dot(a, b):
    return jnp.dot(a, b, preferred_element_type=F32)


def _dot_nt(a, b):
    return lax.dot_general(a, b, (((1,), (1,)), ((), ())), preferred_element_type=F32)


def _dot_tn(a, b):
    return lax.dot_general(a, b, (((0,), (0,)), ((), ())), preferred_element_type=F32)


def _rms_mod(h, g, shift, scale):
    y = h * lax.rsqrt(jnp.mean(h * h, axis=-1, keepdims=True) + EPS) * g
    return y * (1.0 + scale) + shift


def _layer_norm(x, g, b):
    xc = x - jnp.mean(x, axis=-1, keepdims=True)
    y = xc * lax.rsqrt(jnp.mean(xc * xc, axis=-1, keepdims=True) + EPS)
    return y * g + b


def _sigmoid(x):
    return 0.5 * jnp.tanh(0.5 * x) + 0.5


def _silu(x):
    return x * _sigmoid(x)


def _log_sigmoid(z):
    return jnp.minimum(z, 0.0) - jnp.log(1.0 + jnp.exp(-jnp.abs(z)))


def _ada_kernel(c_ref, w_ref, b_ref, o_ref):
    o_ref[...] = _dot(_silu(c_ref[...]), w_ref[...]) + b_ref[...]


def _ada(cond, w_ada, b_ada):
    depth, d, n = w_ada.shape
    bn = n // 4
    return pl.pallas_call(
        _ada_kernel,
        grid=(depth, n // bn),
        in_specs=[
            pl.BlockSpec((8, d), lambda l, j: (0, 0)),
            pl.BlockSpec((None, d, bn), lambda l, j: (l, 0, j)),
            pl.BlockSpec((None, 1, bn), lambda l, j: (l, 0, j)),
        ],
        out_specs=pl.BlockSpec((None, 8, bn), lambda l, j: (l, 0, j)),
        out_shape=jax.ShapeDtypeStruct((depth, 8, n), F32),
        compiler_params=pltpu.CompilerParams(vmem_limit_bytes=VMEM_LIMIT),
        name="ada",
    )(cond, w_ada, b_ada.reshape(depth, 1, n))


def _gla_phases(q, k, v, lg, s_ref, o_ref, *, reverse, n_tok):
    c = GLA_CHUNK
    n_chunks = n_tok // c
    row = lax.broadcasted_iota(jnp.int32, (c, c), 0)
    col = lax.broadcasted_iota(jnp.int32, (c, c), 1)
    if reverse:
        tri = (col >= row).astype(BF16)
        keep = col > row
    else:
        tri = (col <= row).astype(BF16)
        keep = col <= row
    order = range(n_chunks - 1, -1, -1) if reverse else range(n_chunks)
    chunk_rows = [slice(ci * c, (ci + 1) * c) for ci in range(n_chunks)]
    ksls = [slice(hh * GLA_DK, (hh + 1) * GLA_DK) for hh in range(GLA_HEADS)]
    vsls = [slice(hh * GLA_DV, (hh + 1) * GLA_DV) for hh in range(GLA_HEADS)]
    heads = list(zip(ksls, vsls))
    val = {}

    def prep():
        lg_hi = lg.astype(BF16)
        lg_lo = (lg - lg_hi.astype(F32)).astype(BF16)
        cum = _dot(tri, jnp.concatenate([lg_hi[r] for r in chunk_rows] + [lg_lo[r] for r in chunk_rows], axis=1))
        qe, km, kd, dec = [], [], [], []
        for ci, rows in enumerate(chunk_rows):
            b = (cum[:, ci * GLA_KEY:(ci + 1) * GLA_KEY]
                 + cum[:, (n_chunks + ci) * GLA_KEY:(n_chunks + ci + 1) * GLA_KEY])
            b_last = b[0:1] if reverse else b[c - 1:c]
            kc = k[rows]
            qe.append((q[rows] * jnp.exp(b)).astype(BF16))
            km.append((kc * jnp.exp(-b)).astype(BF16))
            kd.append((kc * jnp.exp(b_last - b)).astype(BF16))
            dec.append(jnp.exp(b_last))
        val.update(qe=qe, km=km, kd=kd, dec=dec)

    def scores_and_increments():
        qe, km, kd = val["qe"], val["km"], val["kd"]
        val["scores"] = [[_dot_nt(qe[ci][:, ksl], km[ci][:, ksl]) for ksl, _ in heads] for ci in range(n_chunks)]
        val["upd"] = [[_dot_tn(v[rows, vsl], kd[ci][:, ksl]) for ksl, vsl in heads]
                      for ci, rows in enumerate(chunk_rows)]

    def intra():
        for ci, rows in enumerate(chunk_rows):
            for hh, (_, vsl) in enumerate(heads):
                a = jnp.where(keep, val["scores"][ci][hh], 0.0).astype(BF16)
                o_ref[rows, vsl] = _dot(a, v[rows, vsl])

    def recurrence():
        st = [s_ref[hh] for hh in range(GLA_HEADS)]
        s_in = [None] * n_chunks
        for ci in order:
            s_in[ci] = [s.astype(BF16) for s in st]
            st = [st[hh] * val["dec"][ci][:, ksls[hh]] + val["upd"][ci][hh] for hh in range(GLA_HEADS)]
        for hh in range(GLA_HEADS):
            s_ref[hh] = st[hh]
        val["s_in"] = s_in

    def inter():
        for ci, rows in enumerate(chunk_rows):
            for hh, (ksl, vsl) in enumerate(heads):
                o_ref[rows, vsl] += _dot_nt(val["qe"][ci][:, ksl], val["s_in"][ci][hh])

    return [prep, scores_and_increments, intra, recurrence, inter]


def _glu_to_halo(cin_scr, cpad, c, *, n_seg, seg):
    d = D_MODEL
    cols = slice(c * MXU_COLS, (c + 1) * MXU_COLS)
    hc = cin_scr[:, cols] * _sigmoid(cin_scr[:, d + c * MXU_COLS:d + (c + 1) * MXU_COLS])
    zeros = jnp.zeros((CONV_HALO, MXU_COLS), F32)
    for s in range(n_seg):
        cpad[s, 0:CONV_HALO, cols] = zeros
        cpad[s, CONV_HALO:CONV_HALO + seg, cols] = hc[s * seg:(s + 1) * seg]
        cpad[s, CONV_HALO + seg:2 * CONV_HALO + seg, cols] = zeros


def _conv_lane_tile(cpad, cw_ref, out_ref, s, l0, *, seg):
    first = CONV_HALO - CONV_PAD
    m = seg + SUBLANES
    acc = None
    for b in range(SUBLANES):
        yb = None
        for a in range((CONV_K + first + SUBLANES - 1) // SUBLANES):
            j = SUBLANES * a + b - first
            if 0 <= j < CONV_K:
                term = (cw_ref[j:j + 1, l0:l0 + LANES]
                        * cpad[s, SUBLANES * a:SUBLANES * a + m, l0:l0 + LANES])
                yb = term if yb is None else yb + term
        shifted = yb[b:b + seg]
        acc = shifted if acc is None else acc + shifted
    out_ref[s * seg:(s + 1) * seg, l0:l0 + LANES] = acc


def _gla_bwd_kernel(h_ref, mod_ref, g_ref, w_ref, aup_ref, ab_ref, s0_ref,
                    q_ref, k_ref, v_ref, lgf_ref, ob_ref, sfin_ref, s_scr, *, n_tok):
    t = pl.program_id(1)

    @pl.when(t == 0)
    def _():
        s_scr[...] = s0_ref[...]

    hm = _rms_mod(h_ref[...], g_ref[...], mod_ref[0:1, :], mod_ref[1:2, :]).astype(BF16)
    a_dn = _dot(hm, w_ref[:, QKV_COLS:QKV_COLS + ADN_PAD]).astype(BF16)
    z = _dot(a_dn, aup_ref[...]) + ab_ref[...]
    lg = _log_sigmoid(z) * (1.0 / GLA_GATE_NORM)
    p = _dot(hm, w_ref[:, 0:QKV_COLS])
    q = p[:, 0:GLA_KEY] * (GLA_DK ** -0.5)
    k = p[:, GLA_KEY:2 * GLA_KEY]
    v = p[:, 2 * GLA_KEY:QKV_COLS].astype(BF16)
    q_ref[...] = q.astype(BF16)
    k_ref[...] = k.astype(BF16)
    v_ref[...] = v
    lgf_ref[...] = lg[:, 0:GLA_KEY]
    for phase in _gla_phases(q, k, v, lg[:, GLA_KEY:2 * GLA_KEY], s_scr, ob_ref, reverse=True, n_tok=n_tok):
        phase()

    @pl.when(t == pl.num_programs(1) - 1)
    def _():
        sfin_ref[...] = s_scr[...]


def _gla_fwd_kernel(q_ref, k_ref, v_ref, lgf_ref, ob_ref, s0_ref, on_ref, sfin_ref, s_scr, o_scr, *, n_tok):
    t = pl.program_id(1)

    @pl.when(t == 0)
    def _():
        s_scr[...] = s0_ref[...]

    for phase in _gla_phases(q_ref[...].astype(F32), k_ref[...].astype(F32), v_ref[...], lgf_ref[...], s_scr, o_scr,
                             reverse=False, n_tok=n_tok):
        phase()
    for hh in range(GLA_HEADS):
        vsl = slice(hh * GLA_DV, (hh + 1) * GLA_DV)
        o = o_scr[:, vsl] + ob_ref[:, vsl]
        on_ref[:, vsl] = o * lax.rsqrt(jnp.mean(o * o, axis=-1, keepdims=True) + EPS)

    @pl.when(t == pl.num_programs(1) - 1)
    def _():
        sfin_ref[...] = s_scr[...]


def _const_spec(shape):
    nd = len(shape)
    return pl.BlockSpec(shape, lambda b, t: (0,) * nd, pipeline_mode=pl.Buffered(1))


def _state_spec():
    return pl.BlockSpec((None, GLA_HEADS, GLA_DV, GLA_DK), lambda b, t: (b, 0, 0, 0))


def _gla_bwd(h, mod, g, w_qkva, aup, ab, s0, *, n_tok):
    bsz, seq, d = h.shape
    n_t = seq // n_tok
    rev = lambda b, t: (b, n_t - 1 - t, 0)
    tok = lambda w: pl.BlockSpec((None, n_tok, w), rev)
    return pl.pallas_call(
        functools.partial(_gla_bwd_kernel, n_tok=n_tok),
        grid=(bsz, n_t),
        in_specs=[
            tok(d),
            pl.BlockSpec((None, 8, d), lambda b, t: (b, 0, 0)),
            _const_spec((1, d)),
            _const_spec(w_qkva.shape),
            _const_spec(aup.shape),
            _const_spec(ab.shape),
            _state_spec(),
        ],
        out_specs=[tok(GLA_KEY), tok(GLA_KEY), tok(GLA_VAL), tok(GLA_KEY), tok(GLA_VAL), _state_spec()],
        out_shape=[
            jax.ShapeDtypeStruct((bsz, seq, GLA_KEY), BF16),
            jax.ShapeDtypeStruct((bsz, seq, GLA_KEY), BF16),
            jax.ShapeDtypeStruct((bsz, seq, GLA_VAL), BF16),
            jax.ShapeDtypeStruct((bsz, seq, GLA_KEY), F32),
            jax.ShapeDtypeStruct((bsz, seq, GLA_VAL), F32),
            jax.ShapeDtypeStruct((bsz, GLA_HEADS, GLA_DV, GLA_DK), F32),
        ],
        scratch_shapes=[pltpu.VMEM((GLA_HEADS, GLA_DV, GLA_DK), F32)],
        compiler_params=pltpu.CompilerParams(
            dimension_semantics=("arbitrary", "arbitrary"), vmem_limit_bytes=VMEM_LIMIT),
        name="gla_bwd",
    )(h, mod, g, w_qkva, aup, ab, s0)


def _gla_fwd(q, k, v, lgf, ob, s0, *, n_tok):
    bsz, seq, _ = v.shape
    n_t = seq // n_tok
    tok = lambda w: pl.BlockSpec((None, n_tok, w), lambda b, t: (b, t, 0))
    return pl.pallas_call(
        functools.partial(_gla_fwd_kernel, n_tok=n_tok),
        grid=(bsz, n_t),
        in_specs=[tok(GLA_KEY), tok(GLA_KEY), tok(GLA_VAL), tok(GLA_KEY), tok(GLA_VAL), _state_spec()],
        out_specs=[tok(GLA_VAL), _state_spec()],
        out_shape=[
            jax.ShapeDtypeStruct((bsz, seq, GLA_VAL), F32),
            jax.ShapeDtypeStruct((bsz, GLA_HEADS, GLA_DV, GLA_DK), F32),
        ],
        scratch_shapes=[pltpu.VMEM((GLA_HEADS, GLA_DV, GLA_DK), F32), pltpu.VMEM((n_tok, GLA_VAL), F32)],
        compiler_params=pltpu.CompilerParams(
            dimension_semantics=("arbitrary", "arbitrary"), vmem_limit_bytes=VMEM_LIMIT),
        name="gla_fwd",
    )(q, k, v, lgf, ob, s0)


class _Jobs:
    def __init__(self):
        self._jobs = []

    def add(self, unit, cost, fn, deps=()):
        self._jobs.append((unit, cost, fn, tuple(deps)))
        return len(self._jobs) - 1

    def emit(self):
        free = {"m": 0.0, "v": 0.0}
        start, finish = [], []
        for unit, cost, _, deps in self._jobs:
            t0 = max([free[unit]] + [finish[d] for d in deps])
            start.append(t0)
            finish.append(t0 + cost)
            free[unit] = t0 + cost
        for i in sorted(range(len(self._jobs)), key=lambda i: (start[i], i)):
            self._jobs[i][2]()


def _mixer_kernel(h_ref, on_ref, mod_ref, g_ref, w_r_ref, w_conv_ref, w_sgu_ref, w_gate_ref,
                  gla_g_ref, w_o_gla_ref, cw_ref, cb_ref, cln_g_ref, cln_b_ref, w_o_conv_ref,
                  sln_g_ref, sln_b_ref, ws_ref, sb_ref, w_o_sgu_ref, w_out_ref,
                  out_ref, cpad, cout, cin_scr, r_scr, sg_scr, gt_scr, sp_scr, ya_scr, yb_scr, yc_scr,
                  hm_scr, ain_scr, bin_scr, cin2_scr, yin_scr, *, n_tok, seg):
    d = D_MODEL
    n_seg = n_tok // seg
    n_col = d // MXU_COLS
    col = lambda c: slice(c * MXU_COLS, (c + 1) * MXU_COLS)
    mxu_cost = n_tok
    jobs = _Jobs()
    val = {}

    def f_hm():
        hm_scr[...] = _rms_mod(h_ref[...], g_ref[...], mod_ref[0:1, :], mod_ref[1:2, :]).astype(BF16)
    j_hm = jobs.add("v", 2.5 * n_tok, f_hm)

    def proj(w_ref, dst, c):
        def job():
            dst[:, col(c)] = _dot(hm_scr[...], w_ref[:, col(c)])
        return jobs.add("m", mxu_cost, job, (j_hm,))

    def out_chunk(src, w_ref, dst, c):
        def job():
            dst[:, col(c)] = _dot(src[...], w_ref[:, col(c)])
        return job

    gc = d // SGU_GROUPS
    row_chunks = [slice(n * SGU_CHUNK, (n + 1) * SGU_CHUNK) for n in range(n_tok // SGU_CHUNK)]

    def f_a():
        ain_scr[...] = (on_ref[...] * gla_g_ref[...] * _silu(r_scr[...])).astype(BF16)

    def f_sv(rows):
        def job():
            val["sv", rows.start] = _layer_norm(jax.nn.gelu(sg_scr[rows, d:2 * d]), sln_g_ref[...],
                                                sln_b_ref[...]).astype(BF16)
        return job

    def f_ws(rows):
        def job():
            sv = val["sv", rows.start]
            for gi in range(SGU_GROUPS):
                lsl = slice(gi * gc, (gi + 1) * gc)
                sp_scr[rows, lsl] = _dot(ws_ref[gi], sv[:, lsl]) + sb_ref[:, lsl]
        return job

    def f_c(rows):
        def job():
            cin2_scr[rows, :] = (jax.nn.gelu(sg_scr[rows, 0:d]) * sp_scr[rows, :]).astype(BF16)
        return job

    def f_b(rows):
        def job():
            bin_scr[rows, :] = _silu(_layer_norm(cout[rows, :] + cb_ref[...], cln_g_ref[...],
                                                 cln_b_ref[...])).astype(BF16)
        return job

    def f_m(c):
        def job():
            y = (_sigmoid(gt_scr[:, col(c)]) * ya_scr[:, col(c)]
                 + _sigmoid(gt_scr[:, d + c * MXU_COLS:d + (c + 1) * MXU_COLS]) * yb_scr[:, col(c)]
                 + _sigmoid(gt_scr[:, 2 * d + c * MXU_COLS:2 * d + (c + 1) * MXU_COLS]) * yc_scr[:, col(c)])
            yin_scr[:, col(c)] = y.astype(BF16)
        return job

    def f_o(c):
        def job():
            out_ref[:, col(c)] = h_ref[:, col(c)] + mod_ref[2:3, col(c)] * _dot(yin_scr[...], w_out_ref[:, col(c)])
        return job

    j_conv = []

    def conv_group(c):
        ja = proj(w_conv_ref, cin_scr, c)
        jb = proj(w_conv_ref, cin_scr, n_col + c)
        jg = jobs.add("v", 0.4 * n_tok, lambda: _glu_to_halo(cin_scr, cpad, c, n_seg=n_seg, seg=seg), (ja, jb))
        for l0 in range(c * MXU_COLS, (c + 1) * MXU_COLS, LANES):
            for s in range(n_seg):
                j_conv.append(jobs.add("v", 22 * (seg // SUBLANES),
                                       functools.partial(_conv_lane_tile, cpad, cw_ref, cout, s, l0, seg=seg), (jg,)))

    conv_group(0)
    j_r = [proj(w_r_ref, r_scr, c) for c in range(n_col)]
    j_a = jobs.add("v", 2 * n_tok, f_a, j_r)
    j_sg_hi = [proj(w_sgu_ref, sg_scr, c) for c in range(n_col, 2 * n_col)]
    j_sv = [jobs.add("v", 5 * SGU_CHUNK, f_sv(rows), j_sg_hi) for rows in row_chunks]
    conv_group(1)
    j_ya = [jobs.add("m", mxu_cost, out_chunk(ain_scr, w_o_gla_ref, ya_scr, c), (j_a,)) for c in range(n_col)]
    j_sg_lo = [proj(w_sgu_ref, sg_scr, c) for c in range(n_col)]
    j_ws = [jobs.add("m", 4 * SGU_CHUNK, f_ws(rows), (j,)) for rows, j in zip(row_chunks, j_sv)]
    j_cin = [jobs.add("v", 3.5 * SGU_CHUNK, f_c(rows), [j] + j_sg_lo) for rows, j in zip(row_chunks, j_ws)]
    conv_group(2)
    j_gt = [proj(w_gate_ref, gt_scr, c) for c in range(3 * n_col // 2)]
    conv_group(3)
    j_gt += [proj(w_gate_ref, gt_scr, c) for c in range(3 * n_col // 2, 3 * n_col)]
    j_yc = [jobs.add("m", mxu_cost, out_chunk(cin2_scr, w_o_sgu_ref, yc_scr, c), j_cin) for c in range(n_col)]
    j_bin = [jobs.add("v", 4 * SGU_CHUNK, f_b(rows), j_conv) for rows in row_chunks]
    j_yb = [jobs.add("m", mxu_cost, out_chunk(bin_scr, w_o_conv_ref, yb_scr, c), j_bin) for c in range(n_col)]
    j_mg = [jobs.add("v", 1.2 * n_tok, f_m(c),
                     (j_gt[c], j_gt[n_col + c], j_gt[2 * n_col + c], j_ya[c], j_yb[c], j_yc[c]))
            for c in range(n_col)]
    for c in range(n_col):
        jobs.add("m", mxu_cost, f_o(c), j_mg)

    jobs.emit()


def _mixer(h, on, mod, g, wts, *, n_tok, seg):
    bsz, seq, d = h.shape
    n_t = seq // n_tok
    tok = pl.BlockSpec((None, n_tok, d), lambda b, t: (b, t, 0))
    f32s = lambda w: pltpu.VMEM((n_tok, w), F32)
    bf16s = lambda w: pltpu.VMEM((n_tok, w), BF16)
    return pl.pallas_call(
        functools.partial(_mixer_kernel, n_tok=n_tok, seg=seg),
        grid=(bsz, n_t),
        in_specs=[tok, tok, pl.BlockSpec((None, 8, d), lambda b, t: (b, 0, 0)), _const_spec((1, d))]
        + [_const_spec(w.shape) for w in wts],
        out_specs=tok,
        out_shape=jax.ShapeDtypeStruct((bsz, seq, d), F32),
        scratch_shapes=[
            pltpu.VMEM((n_tok // seg, seg + 2 * CONV_HALO, d), F32),
            f32s(d), f32s(2 * d), f32s(d), f32s(2 * d), f32s(3 * d), f32s(d),
            f32s(d), f32s(d), f32s(d),
            bf16s(d), bf16s(d), bf16s(d), bf16s(d), bf16s(d),
        ],
        compiler_params=pltpu.CompilerParams(
            dimension_semantics=("parallel", "parallel"), vmem_limit_bytes=VMEM_LIMIT),
        name="mixer",
    )(h, on, mod, g, *wts)


def _ffn_kernel(*refs, final):
    if final:
        h_ref, mod_ref, g_ref, w1_ref, w2_ref, fg_ref, out_ref = refs
    else:
        h_ref, mod_ref, g_ref, w1_ref, w2_ref, out_ref = refs
    h = h_ref[...]
    hm = _rms_mod(h, g_ref[...], mod_ref[3:4, :], mod_ref[4:5, :]).astype(BF16)
    gu = _dot(hm, w1_ref[...])
    act = (_silu(gu[:, 0:D_FF]) * gu[:, D_FF:2 * D_FF]).astype(BF16)
    out = h + mod_ref[5:6, :] * _dot(act, w2_ref[...])
    if final:
        out = out * lax.rsqrt(jnp.mean(out * out, axis=-1, keepdims=True) + EPS) * fg_ref[...]
    out_ref[...] = out


def _ffn(h, mod, g, w1, w2, final_g=None, *, n_tok):
    bsz, seq, d = h.shape
    tok = pl.BlockSpec((None, n_tok, d), lambda b, t: (b, t, 0))
    final = final_g is not None
    extra = (final_g,) if final else ()
    return pl.pallas_call(
        functools.partial(_ffn_kernel, final=final),
        grid=(bsz, seq // n_tok),
        in_specs=[tok, pl.BlockSpec((None, 8, d), lambda b, t: (b, 0, 0)), _const_spec((1, d)),
                  _const_spec(w1.shape), _const_spec(w2.shape)] + [_const_spec((1, d)) for _ in extra],
        out_specs=tok,
        out_shape=jax.ShapeDtypeStruct((bsz, seq, d), F32),
        compiler_params=pltpu.CompilerParams(
            dimension_semantics=("parallel", "parallel"), vmem_limit_bytes=VMEM_LIMIT),
        name="ffn",
    )(h, mod, g, w1, w2, *extra)


def _pick_tile(seq, want):
    return min(seq, want)


def kernel(x, c, ctx, c_ctx, w_ada, b_ada, norm1_g, norm2_g, w_in, gla_a_up, gla_a_b, gla_norm_g, w_o_gla,
           conv_w, conv_b, conv_ln_g, conv_ln_b, w_o_conv, sgu_ln_g, sgu_ln_b, sgu_ws, sgu_b, w_o_sgu, w_out,
           w_ffn_in, w_ffn_out, final_g):
    bsz, seq, d = x.shape
    depth = w_ada.shape[0]
    ctx_len = ctx.shape[1]

    cond = jnp.zeros((8, d), F32).at[0:bsz].set(c).at[bsz].set(c_ctx)
    ada = _ada(cond, w_ada, b_ada).reshape(depth, 8, 6, d)
    ada = jnp.pad(ada, ((0, 0), (0, 0), (0, 2), (0, 0)))

    row = lambda a: a.reshape(1, -1)
    c0 = 0
    offs = []
    for wdt in (GLA_KEY, GLA_KEY, GLA_VAL, 2 * GLA_RANK, GLA_VAL, 2 * D_MODEL, 2 * D_MODEL, 3 * D_MODEL):
        offs.append((c0, c0 + wdt))
        c0 += wdt
    (_, _), (_, _), (_, v_end), (a_lo, a_hi), (r_lo, r_hi), (cv_lo, cv_hi), (sg_lo, sg_hi), (gt_lo, gt_hi) = offs

    h_lat, h_ctx = x, ctx
    zero_state = jnp.zeros((bsz, GLA_HEADS, GLA_DV, GLA_DK), F32)
    t_lat = _pick_tile(seq, 512)
    t_mix = _pick_tile(seq, 256)
    for l in range(depth):
        wl = w_in[l]
        w_qkva = jnp.pad(wl[:, 0:a_hi], ((0, 0), (0, ADN_PAD - (a_hi - a_lo)))).astype(BF16)
        aup = jnp.zeros((ADN_PAD, 2 * GLA_KEY), F32)
        aup = aup.at[0:GLA_RANK, 0:GLA_KEY].set(gla_a_up[l, 0])
        aup = aup.at[GLA_RANK:2 * GLA_RANK, GLA_KEY:2 * GLA_KEY].set(gla_a_up[l, 1]).astype(BF16)
        ab = gla_a_b[l].reshape(1, 2 * GLA_KEY)
        sgu_bias = jnp.repeat(sgu_b[l].T, D_MODEL // SGU_GROUPS, axis=1)
        mix_w = (
            wl[:, r_lo:r_hi].astype(BF16), wl[:, cv_lo:cv_hi].astype(BF16), wl[:, sg_lo:sg_hi].astype(BF16),
            wl[:, gt_lo:gt_hi].astype(BF16),
            row(gla_norm_g[l]), w_o_gla[l].astype(BF16),
            jnp.pad(conv_w[l], ((0, 1), (0, 0))), row(conv_b[l]), row(conv_ln_g[l]), row(conv_ln_b[l]),
            w_o_conv[l].astype(BF16),
            row(sgu_ln_g[l]), row(sgu_ln_b[l]), sgu_ws[l].astype(BF16), sgu_bias, w_o_sgu[l].astype(BF16),
            w_out[l].astype(BF16),
        )
        w1 = w_ffn_in[l].astype(BF16)
        w2 = w_ffn_out[l].astype(BF16)
        g1, g2 = row(norm1_g[l]), row(norm2_g[l])
        mod_lat = ada[l, 0:bsz]
        mod_ctx = jnp.broadcast_to(ada[l, bsz:bsz + 1], (bsz, 8, d))

        qc, kc, vc, lgc, obc, s_b = _gla_bwd(h_ctx, mod_ctx, g1, w_qkva, aup, ab, zero_state, n_tok=ctx_len)
        onc, s_f = _gla_fwd(qc, kc, vc, lgc, obc, zero_state, n_tok=ctx_len)
        if l < depth - 1:
            h_ctx = _mixer(h_ctx, onc, mod_ctx, g1, mix_w, n_tok=ctx_len, seg=ctx_len)
            h_ctx = _ffn(h_ctx, mod_ctx, g2, w1, w2, n_tok=ctx_len)

        q, k, v, lgf, ob, _ = _gla_bwd(h_lat, mod_lat, g1, w_qkva, aup, ab, s_b, n_tok=t_lat)
        on, _ = _gla_fwd(q, k, v, lgf, ob, s_f, n_tok=t_lat)
        h_lat = _mixer(h_lat, on, mod_lat, g1, mix_w, n_tok=t_mix, seg=GRID_W)
        h_lat = _ffn(h_lat, mod_lat, g2, w1, w2, row(final_g) if l == depth - 1 else None, n_tok=t_lat)
    return h_lat
```

```python
import functools

import jax
import jax.numpy as jnp
from jax import lax
from jax.experimental import pallas as pl
from jax.experimental.pallas import tpu as pltpu

F32 = jnp.float32
BF16 = jnp.bfloat16

D_MODEL = 1024
GRID_W = 64
GLA_HEADS = 4
GLA_KEY = D_MODEL // 2
GLA_VAL = D_MODEL
GLA_DK = GLA_KEY // GLA_HEADS
GLA_DV = GLA_VAL // GLA_HEADS
GLA_RANK = 16
GLA_CHUNK = 64
GLA_GATE_NORM = 16.0
CONV_K = 31
CONV_PAD = CONV_K // 2
SGU_GROUPS = 8
SGU_CHUNK = 128
D_FF = ((8 * D_MODEL + 3 * 256 - 1) // (3 * 256)) * 256
EPS = 1e-6

LANES = 128
SUBLANES = 8
MXU_COLS = 256
QKV_COLS = 2 * GLA_KEY + GLA_VAL
ADN_PAD = LANES
CONV_HALO = 16
VMEM_LIMIT = 56 * 1024 * 1024


def _dot(a, b):
    return jnp.dot(a, b, preferred_element_type=F32)


def _dot_nt(a, b):
    return lax.dot_general(a, b, (((1,), (1,)), ((), ())), preferred_element_type=F32)


def _dot_tn(a, b):
    return lax.dot_general(a, b, (((0,), (0,)), ((), ())), preferred_element_type=F32)


def _rms_mod(h, g, shift, scale):
    y = h * lax.rsqrt(jnp.mean(h * h, axis=-1, keepdims=True) + EPS) * g
    return y * (1.0 + scale) + shift


def _layer_norm(x, g, b):
    xc = x - jnp.mean(x, axis=-1, keepdims=True)
    y = xc * lax.rsqrt(jnp.mean(xc * xc, axis=-1, keepdims=True) + EPS)
    return y * g + b


def _sigmoid(x):
    return 0.5 * jnp.tanh(0.5 * x) + 0.5


def _silu(x):
    return x * _sigmoid(x)


def _log_sigmoid(z):
    return jnp.minimum(z, 0.0) - jnp.log(1.0 + jnp.exp(-jnp.abs(z)))


def _ada_kernel(c_ref, w_ref, b_ref, o_ref):
    o_ref[...] = _dot(_silu(c_ref[...]), w_ref[...]) + b_ref[...]


def _ada(cond, w_ada, b_ada):
    depth, d, n = w_ada.shape
    bn = n // 4
    return pl.pallas_call(
        _ada_kernel,
        grid=(depth, n // bn),
        in_specs=[
            pl.BlockSpec((8, d), lambda l, j: (0, 0)),
            pl.BlockSpec((None, d, bn), lambda l, j: (l, 0, j)),
            pl.BlockSpec((None, 1, bn), lambda l, j: (l, 0, j)),
        ],
        out_specs=pl.BlockSpec((None, 8, bn), lambda l, j: (l, 0, j)),
        out_shape=jax.ShapeDtypeStruct((depth, 8, n), F32),
        compiler_params=pltpu.CompilerParams(vmem_limit_bytes=VMEM_LIMIT),
        name="ada",
    )(cond, w_ada, b_ada.reshape(depth, 1, n))


def _gla_phases(q, k, v, lg, s_ref, o_ref, *, reverse, n_tok):
    c = GLA_CHUNK
    n_chunks = n_tok // c
    row = lax.broadcasted_iota(jnp.int32, (c, c), 0)
    col = lax.broadcasted_iota(jnp.int32, (c, c), 1)
    if reverse:
        tri = (col >= row).astype(BF16)
        keep = col > row
    else:
        tri = (col <= row).astype(BF16)
        keep = col <= row
    order = range(n_chunks - 1, -1, -1) if reverse else range(n_chunks)
    chunk_rows = [slice(ci * c, (ci + 1) * c) for ci in range(n_chunks)]
    ksls = [slice(hh * GLA_DK, (hh + 1) * GLA_DK) for hh in range(GLA_HEADS)]
    vsls = [slice(hh * GLA_DV, (hh + 1) * GLA_DV) for hh in range(GLA_HEADS)]
    heads = list(zip(ksls, vsls))
    val = {}

    def prep():
        lg_hi = lg.astype(BF16)
        lg_lo = (lg - lg_hi.astype(F32)).astype(BF16)
        cum = _dot(tri, jnp.concatenate([lg_hi[r] for r in chunk_rows] + [lg_lo[r] for r in chunk_rows], axis=1))
        qe, km, kd, dec = [], [], [], []
        for ci, rows in enumerate(chunk_rows):
            b = (cum[:, ci * GLA_KEY:(ci + 1) * GLA_KEY]
                 + cum[:, (n_chunks + ci) * GLA_KEY:(n_chunks + ci + 1) * GLA_KEY])
            b_last = b[0:1] if reverse else b[c - 1:c]
            kc = k[rows]
            qe.append((q[rows] * jnp.exp(b)).astype(BF16))
            km.append((kc * jnp.exp(-b)).astype(BF16))
            kd.append((kc * jnp.exp(b_last - b)).astype(BF16))
            dec.append(jnp.exp(b_last))
        val.update(qe=qe, km=km, kd=kd, dec=dec)

    def scores_and_increments():
        qe, km, kd = val["qe"], val["km"], val["kd"]
        val["scores"] = [[_dot_nt(qe[ci][:, ksl], km[ci][:, ksl]) for ksl, _ in heads] for ci in range(n_chunks)]
        val["upd"] = [[_dot_tn(v[rows, vsl], kd[ci][:, ksl]) for ksl, vsl in heads]
                      for ci, rows in enumerate(chunk_rows)]

    def intra():
        for ci, rows in enumerate(chunk_rows):
            for hh, (_, vsl) in enumerate(heads):
                a = jnp.where(keep, val["scores"][ci][hh], 0.0).astype(BF16)
                o_ref[rows, vsl] = _dot(a, v[rows, vsl])

    def recurrence():
        st = [s_ref[hh] for hh in range(GLA_HEADS)]
        s_in = [None] * n_chunks
        for ci in order:
            s_in[ci] = [s.astype(BF16) for s in st]
            st = [st[hh] * val["dec"][ci][:, ksls[hh]] + val["upd"][ci][hh] for hh in range(GLA_HEADS)]
        for hh in range(GLA_HEADS):
            s_ref[hh] = st[hh]
        val["s_in"] = s_in

    def inter():
        for ci, rows in enumerate(chunk_rows):
            for hh, (ksl, vsl) in enumerate(heads):
                o_ref[rows, vsl] += _dot_nt(val["qe"][ci][:, ksl], val["s_in"][ci][hh])

    return [prep, scores_and_increments, intra, recurrence, inter]


def _glu_to_halo(cin_scr, cpad, c, *, n_seg, seg):
    d = D_MODEL
    cols = slice(c * MXU_COLS, (c + 1) * MXU_COLS)
    hc = cin_scr[:, cols] * _sigmoid(cin_scr[:, d + c * MXU_COLS:d + (c + 1) * MXU_COLS])
    zeros = jnp.zeros((CONV_HALO, MXU_COLS), F32)
    for s in range(n_seg):
        cpad[s, 0:CONV_HALO, cols] = zeros
        cpad[s, CONV_HALO:CONV_HALO + seg, cols] = hc[s * seg:(s + 1) * seg]
        cpad[s, CONV_HALO + seg:2 * CONV_HALO + seg, cols] = zeros


def _conv_lane_tile(cpad, cw_ref, out_ref, s, l0, *, seg):
    first = CONV_HALO - CONV_PAD
    m = seg + SUBLANES
    acc = None
    for b in range(SUBLANES):
        yb = None
        for a in range((CONV_K + first + SUBLANES - 1) // SUBLANES):
            j = SUBLANES * a + b - first
            if 0 <= j < CONV_K:
                term = (cw_ref[j:j + 1, l0:l0 + LANES]
                        * cpad[s, SUBLANES * a:SUBLANES * a + m, l0:l0 + LANES])
                yb = term if yb is None else yb + term
        shifted = yb[b:b + seg]
        acc = shifted if acc is None else acc + shifted
    out_ref[s * seg:(s + 1) * seg, l0:l0 + LANES] = acc


def _gla_bwd_kernel(h_ref, mod_ref, g_ref, w_ref, aup_ref, ab_ref, s0_ref,
                    q_ref, k_ref, v_ref, lgf_ref, ob_ref, sfin_ref, s_scr, *, n_tok):
    t = pl.program_id(1)

    @pl.when(t == 0)
    def _():
        s_scr[...] = s0_ref[...]

    hm = _rms_mod(h_ref[...], g_ref[...], mod_ref[0:1, :], mod_ref[1:2, :]).astype(BF16)
    a_dn = _dot(hm, w_ref[:, QKV_COLS:QKV_COLS + ADN_PAD]).astype(BF16)
    z = _dot(a_dn, aup_ref[...]) + ab_ref[...]
    lg = _log_sigmoid(z) * (1.0 / GLA_GATE_NORM)
    p = _dot(hm, w_ref[:, 0:QKV_COLS])
    q = p[:, 0:GLA_KEY] * (GLA_DK ** -0.5)
    k = p[:, GLA_KEY:2 * GLA_KEY]
    v = p[:, 2 * GLA_KEY:QKV_COLS].astype(BF16)
    q_ref[...] = q.astype(BF16)
    k_ref[...] = k.astype(BF16)
    v_ref[...] = v
    lgf_ref[...] = lg[:, 0:GLA_KEY]
    for phase in _gla_phases(q, k, v, lg[:, GLA_KEY:2 * GLA_KEY], s_scr, ob_ref, reverse=True, n_tok=n_tok):
        phase()

    @pl.when(t == pl.num_programs(1) - 1)
    def _():
        sfin_ref[...] = s_scr[...]


def _gla_fwd_kernel(q_ref, k_ref, v_ref, lgf_ref, ob_ref, s0_ref, on_ref, sfin_ref, s_scr, o_scr, *, n_tok):
    t = pl.program_id(1)

    @pl.when(t == 0)
    def _():
        s_scr[...] = s0_ref[...]

    for phase in _gla_phases(q_ref[...].astype(F32), k_ref[...].astype(F32), v_ref[...], lgf_ref[...], s_scr, o_scr,
                             reverse=False, n_tok=n_tok):
        phase()
    for hh in range(GLA_HEADS):
        vsl = slice(hh * GLA_DV, (hh + 1) * GLA_DV)
        o = o_scr[:, vsl] + ob_ref[:, vsl]
        on_ref[:, vsl] = o * lax.rsqrt(jnp.mean(o * o, axis=-1, keepdims=True) + EPS)

    @pl.when(t == pl.num_programs(1) - 1)
    def _():
        sfin_ref[...] = s_scr[...]


def _const_spec(shape):
    nd = len(shape)
    return pl.BlockSpec(shape, lambda b, t: (0,) * nd, pipeline_mode=pl.Buffered(1))


def _state_spec():
    return pl.BlockSpec((None, GLA_HEADS, GLA_DV, GLA_DK), lambda b, t: (b, 0, 0, 0))


def _gla_bwd(h, mod, g, w_qkva, aup, ab, s0, *, n_tok):
    bsz, seq, d = h.shape
    n_t = seq // n_tok
    rev = lambda b, t: (b, n_t - 1 - t, 0)
    tok = lambda w: pl.BlockSpec((None, n_tok, w), rev)
    return pl.pallas_call(
        functools.partial(_gla_bwd_kernel, n_tok=n_tok),
        grid=(bsz, n_t),
        in_specs=[
            tok(d),
            pl.BlockSpec((None, 8, d), lambda b, t: (b, 0, 0)),
            _const_spec((1, d)),
            _const_spec(w_qkva.shape),
            _const_spec(aup.shape),
            _const_spec(ab.shape),
            _state_spec(),
        ],
        out_specs=[tok(GLA_KEY), tok(GLA_KEY), tok(GLA_VAL), tok(GLA_KEY), tok(GLA_VAL), _state_spec()],
        out_shape=[
            jax.ShapeDtypeStruct((bsz, seq, GLA_KEY), BF16),
            jax.ShapeDtypeStruct((bsz, seq, GLA_KEY), BF16),
            jax.ShapeDtypeStruct((bsz, seq, GLA_VAL), BF16),
            jax.ShapeDtypeStruct((bsz, seq, GLA_KEY), F32),
            jax.ShapeDtypeStruct((bsz, seq, GLA_VAL), F32),
            jax.ShapeDtypeStruct((bsz, GLA_HEADS, GLA_DV, GLA_DK), F32),
        ],
        scratch_shapes=[pltpu.VMEM((GLA_HEADS, GLA_DV, GLA_DK), F32)],
        compiler_params=pltpu.CompilerParams(
            dimension_semantics=("arbitrary", "arbitrary"), vmem_limit_bytes=VMEM_LIMIT),
        name="gla_bwd",
    )(h, mod, g, w_qkva, aup, ab, s0)


def _gla_fwd(q, k, v, lgf, ob, s0, *, n_tok):
    bsz, seq, _ = v.shape
    n_t = seq // n_tok
    tok = lambda w: pl.BlockSpec((None, n_tok, w), lambda b, t: (b, t, 0))
    return pl.pallas_call(
        functools.partial(_gla_fwd_kernel, n_tok=n_tok),
        grid=(bsz, n_t),
        in_specs=[tok(GLA_KEY), tok(GLA_KEY), tok(GLA_VAL), tok(GLA_KEY), tok(GLA_VAL), _state_spec()],
        out_specs=[tok(GLA_VAL), _state_spec()],
        out_shape=[
            jax.ShapeDtypeStruct((bsz, seq, GLA_VAL), F32),
            jax.ShapeDtypeStruct((bsz, GLA_HEADS, GLA_DV, GLA_DK), F32),
        ],
        scratch_shapes=[pltpu.VMEM((GLA_HEADS, GLA_DV, GLA_DK), F32), pltpu.VMEM((n_tok, GLA_VAL), F32)],
        compiler_params=pltpu.CompilerParams(
            dimension_semantics=("arbitrary", "arbitrary"), vmem_limit_bytes=VMEM_LIMIT),
        name="gla_fwd",
    )(q, k, v, lgf, ob, s0)


class _Jobs:
    def __init__(self):
        self._jobs = []

    def add(self, unit, cost, fn, deps=()):
        self._jobs.append((unit, cost, fn, tuple(deps)))
        return len(self._jobs) - 1

    def emit(self):
        free = {"m": 0.0, "v": 0.0}
        start, finish = [], []
        for unit, cost, _, deps in self._jobs:
            t0 = max([free[unit]] + [finish[d] for d in deps])
            start.append(t0)
            finish.append(t0 + cost)
            free[unit] = t0 + cost
        for i in sorted(range(len(self._jobs)), key=lambda i: (start[i], i)):
            self._jobs[i][2]()


def _mixer_kernel(h_ref, on_ref, mod_ref, g_ref, w_r_ref, w_conv_ref, w_sgu_ref, w_gate_ref,
                  gla_g_ref, w_o_gla_ref, cw_ref, cb_ref, cln_g_ref, cln_b_ref, w_o_conv_ref,
                  sln_g_ref, sln_b_ref, ws_ref, sb_ref, w_o_sgu_ref, w_out_ref,
                  out_ref, cpad, cout, cin_scr, r_scr, sg_scr, gt_scr, sp_scr, ya_scr, yb_scr, yc_scr,
                  hm_scr, ain_scr, bin_scr, cin2_scr, yin_scr, *, n_tok, seg):
    d = D_MODEL
    n_seg = n_tok // seg
    n_col = d // MXU_COLS
    col = lambda c: slice(c * MXU_COLS, (c + 1) * MXU_COLS)
    mxu_cost = n_tok
    jobs = _Jobs()
    val = {}

    def f_hm():
        hm_scr[...] = _rms_mod(h_ref[...], g_ref[...], mod_ref[0:1, :], mod_ref[1:2, :]).astype(BF16)
    j_hm = jobs.add("v", 2.5 * n_tok, f_hm)

    def proj(w_ref, dst, c):
        def job():
            dst[:, col(c)] = _dot(hm_scr[...], w_ref[:, col(c)])
        return jobs.add("m", mxu_cost, job, (j_hm,))

    def out_chunk(src, w_ref, dst, c):
        def job():
            dst[:, col(c)] = _dot(src[...], w_ref[:, col(c)])
        return job

    gc = d // SGU_GROUPS
    row_chunks = [slice(n * SGU_CHUNK, (n + 1) * SGU_CHUNK) for n in range(n_tok // SGU_CHUNK)]

    def f_a():
        ain_scr[...] = (on_ref[...] * gla_g_ref[...] * _silu(r_scr[...])).astype(BF16)

    def f_sv(rows):
        def job():
            val["sv", rows.start] = _layer_norm(jax.nn.gelu(sg_scr[rows, d:2 * d]), sln_g_ref[...],
                                                sln_b_ref[...]).astype(BF16)
        return job

    def f_ws(rows):
        def job():
            sv = val["sv", rows.start]
            for gi in range(SGU_GROUPS):
                lsl = slice(gi * gc, (gi + 1) * gc)
                sp_scr[rows, lsl] = _dot(ws_ref[gi], sv[:, lsl]) + sb_ref[:, lsl]
        return job

    def f_c(rows):
        def job():
            cin2_scr[rows, :] = (jax.nn.gelu(sg_scr[rows, 0:d]) * sp_scr[rows, :]).astype(BF16)
        return job

    def f_b(rows):
        def job():
            bin_scr[rows, :] = _silu(_layer_norm(cout[rows, :] + cb_ref[...], cln_g_ref[...],
                                                 cln_b_ref[...])).astype(BF16)
        return job

    def f_m(c):
        def job():
            y = (_sigmoid(gt_scr[:, col(c)]) * ya_scr[:, col(c)]
                 + _sigmoid(gt_scr[:, d + c * MXU_COLS:d + (c + 1) * MXU_COLS]) * yb_scr[:, col(c)]
                 + _sigmoid(gt_scr[:, 2 * d + c * MXU_COLS:2 * d + (c + 1) * MXU_COLS]) * yc_scr[:, col(c)])
            yin_scr[:, col(c)] = y.astype(BF16)
        return job

    def f_o(c):
        def job():
            out_ref[:, col(c)] = h_ref[:, col(c)] + mod_ref[2:3, col(c)] * _dot(yin_scr[...], w_out_ref[:, col(c)])
        return job

    j_conv = []

    def conv_group(c):
        ja = proj(w_conv_ref, cin_scr, c)
        jb = proj(w_conv_ref, cin_scr, n_col + c)
        jg = jobs.add("v", 0.4 * n_tok, lambda: _glu_to_halo(cin_scr, cpad, c, n_seg=n_seg, seg=seg), (ja, jb))
        for l0 in range(c * MXU_COLS, (c + 1) * MXU_COLS, LANES):
            for s in range(n_seg):
                j_conv.append(jobs.add("v", 22 * (seg // SUBLANES),
                                       functools.partial(_conv_lane_tile, cpad, cw_ref, cout, s, l0, seg=seg), (jg,)))

    conv_group(0)
    j_r = [proj(w_r_ref, r_scr, c) for c in range(n_col)]
    j_a = jobs.add("v", 2 * n_tok, f_a, j_r)
    j_sg_hi = [proj(w_sgu_ref, sg_scr, c) for c in range(n_col, 2 * n_col)]
    j_sv = [jobs.add("v", 5 * SGU_CHUNK, f_sv(rows), j_sg_hi) for rows in row_chunks]
    conv_group(1)
    j_ya = [jobs.add("m", mxu_cost, out_chunk(ain_scr, w_o_gla_ref, ya_scr, c), (j_a,)) for c in range(n_col)]
    j_sg_lo = [proj(w_sgu_ref, sg_scr, c) for c in range(n_col)]
    j_ws = [jobs.add("m", 4 * SGU_CHUNK, f_ws(rows), (j,)) for rows, j in zip(row_chunks, j_sv)]
    j_cin = [jobs.add("v", 3.5 * SGU_CHUNK, f_c(rows), [j] + j_sg_lo) for rows, j in zip(row_chunks, j_ws)]
    conv_group(2)
    j_gt = [proj(w_gate_ref, gt_scr, c) for c in range(3 * n_col // 2)]
    conv_group(3)
    j_gt += [proj(w_gate_ref, gt_scr, c) for c in range(3 * n_col // 2, 3 * n_col)]
    j_yc = [jobs.add("m", mxu_cost, out_chunk(cin2_scr, w_o_sgu_ref, yc_scr, c), j_cin) for c in range(n_col)]
    j_bin = [jobs.add("v", 4 * SGU_CHUNK, f_b(rows), j_conv) for rows in row_chunks]
    j_yb = [jobs.add("m", mxu_cost, out_chunk(bin_scr, w_o_conv_ref, yb_scr, c), j_bin) for c in range(n_col)]
    j_mg = [jobs.add("v", 1.2 * n_tok, f_m(c),
                     (j_gt[c], j_gt[n_col + c], j_gt[2 * n_col + c], j_ya[c], j_yb[c], j_yc[c]))
            for c in range(n_col)]
    for c in range(n_col):
        jobs.add("m", mxu_cost, f_o(c), j_mg)

    jobs.emit()


def _mixer(h, on, mod, g, wts, *, n_tok, seg):
    bsz, seq, d = h.shape
    n_t = seq // n_tok
    tok = pl.BlockSpec((None, n_tok, d), lambda b, t: (b, t, 0))
    f32s = lambda w: pltpu.VMEM((n_tok, w), F32)
    bf16s = lambda w: pltpu.VMEM((n_tok, w), BF16)
    return pl.pallas_call(
        functools.partial(_mixer_kernel, n_tok=n_tok, seg=seg),
        grid=(bsz, n_t),
        in_specs=[tok, tok, pl.BlockSpec((None, 8, d), lambda b, t: (b, 0, 0)), _const_spec((1, d))]
        + [_const_spec(w.shape) for w in wts],
        out_specs=tok,
        out_shape=jax.ShapeDtypeStruct((bsz, seq, d), F32),
        scratch_shapes=[
            pltpu.VMEM((n_tok // seg, seg + 2 * CONV_HALO, d), F32),
            f32s(d), f32s(2 * d), f32s(d), f32s(2 * d), f32s(3 * d), f32s(d),
            f32s(d), f32s(d), f32s(d),
            bf16s(d), bf16s(d), bf16s(d), bf16s(d), bf16s(d),
        ],
        compiler_params=pltpu.CompilerParams(
            dimension_semantics=("parallel", "parallel"), vmem_limit_bytes=VMEM_LIMIT),
        name="mixer",
    )(h, on, mod, g, *wts)


def _ffn_kernel(*refs, final):
    if final:
        h_ref, mod_ref, g_ref, w1_ref, w2_ref, fg_ref, out_ref = refs
    else:
        h_ref, mod_ref, g_ref, w1_ref, w2_ref, out_ref = refs
    h = h_ref[...]
    hm = _rms_mod(h, g_ref[...], mod_ref[3:4, :], mod_ref[4:5, :]).astype(BF16)
    gu = _dot(hm, w1_ref[...])
    act = (_silu(gu[:, 0:D_FF]) * gu[:, D_FF:2 * D_FF]).astype(BF16)
    out = h + mod_ref[5:6, :] * _dot(act, w2_ref[...])
    if final:
        out = out * lax.rsqrt(jnp.mean(out * out, axis=-1, keepdims=True) + EPS) * fg_ref[...]
    out_ref[...] = out


def _ffn(h, mod, g, w1, w2, final_g=None, *, n_tok):
    bsz, seq, d = h.shape
    tok = pl.BlockSpec((None, n_tok, d), lambda b, t: (b, t, 0))
    final = final_g is not None
    extra = (final_g,) if final else ()
    return pl.pallas_call(
        functools.partial(_ffn_kernel, final=final),
        grid=(bsz, seq // n_tok),
        in_specs=[tok, pl.BlockSpec((None, 8, d), lambda b, t: (b, 0, 0)), _const_spec((1, d)),
                  _const_spec(w1.shape), _const_spec(w2.shape)] + [_const_spec((1, d)) for _ in extra],
        out_specs=tok,
        out_shape=jax.ShapeDtypeStruct((bsz, seq, d), F32),
        compiler_params=pltpu.CompilerParams(
            dimension_semantics=("parallel", "parallel"), vmem_limit_bytes=VMEM_LIMIT),
        name="ffn",
    )(h, mod, g, w1, w2, *extra)


def _pick_tile(seq, want):
    return min(seq, want)


def kernel(x, c, ctx, c_ctx, w_ada, b_ada, norm1_g, norm2_g, w_in, gla_a_up, gla_a_b, gla_norm_g, w_o_gla,
           conv_w, conv_b, conv_ln_g, conv_ln_b, w_o_conv, sgu_ln_g, sgu_ln_b, sgu_ws, sgu_b, w_o_sgu, w_out,
           w_ffn_in, w_ffn_out, final_g):
    bsz, seq, d = x.shape
    depth = w_ada.shape[0]
    ctx_len = ctx.shape[1]

    cond = jnp.zeros((8, d), F32).at[0:bsz].set(c).at[bsz].set(c_ctx)
    ada = _ada(cond, w_ada, b_ada).reshape(depth, 8, 6, d)
    ada = jnp.pad(ada, ((0, 0), (0, 0), (0, 2), (0, 0)))

    row = lambda a: a.reshape(1, -1)
    c0 = 0
    offs = []
    for wdt in (GLA_KEY, GLA_KEY, GLA_VAL, 2 * GLA_RANK, GLA_VAL, 2 * D_MODEL, 2 * D_MODEL, 3 * D_MODEL):
        offs.append((c0, c0 + wdt))
        c0 += wdt
    (_, _), (_, _), (_, v_end), (a_lo, a_hi), (r_lo, r_hi), (cv_lo, cv_hi), (sg_lo, sg_hi), (gt_lo, gt_hi) = offs

    h_lat, h_ctx = x, ctx
    zero_state = jnp.zeros((bsz, GLA_HEADS, GLA_DV, GLA_DK), F32)
    t_scan = _pick_tile(seq, 1024)
    t_lat = _pick_tile(seq, 512)
    t_mix = _pick_tile(seq, 256)
    for l in range(depth):
        wl = w_in[l]
        w_qkva = jnp.pad(wl[:, 0:a_hi], ((0, 0), (0, ADN_PAD - (a_hi - a_lo)))).astype(BF16)
        aup = jnp.zeros((ADN_PAD, 2 * GLA_KEY), F32)
        aup = aup.at[0:GLA_RANK, 0:GLA_KEY].set(gla_a_up[l, 0])
        aup = aup.at[GLA_RANK:2 * GLA_RANK, GLA_KEY:2 * GLA_KEY].set(gla_a_up[l, 1]).astype(BF16)
        ab = gla_a_b[l].reshape(1, 2 * GLA_KEY)
        sgu_bias = jnp.repeat(sgu_b[l].T, D_MODEL // SGU_GROUPS, axis=1)
        mix_w = (
            wl[:, r_lo:r_hi].astype(BF16), wl[:, cv_lo:cv_hi].astype(BF16), wl[:, sg_lo:sg_hi].astype(BF16),
            wl[:, gt_lo:gt_hi].astype(BF16),
            row(gla_norm_g[l]), w_o_gla[l].astype(BF16),
            jnp.pad(conv_w[l], ((0, 1), (0, 0))), row(conv_b[l]), row(conv_ln_g[l]), row(conv_ln_b[l]),
            w_o_conv[l].astype(BF16),
            row(sgu_ln_g[l]), row(sgu_ln_b[l]), sgu_ws[l].astype(BF16), sgu_bias, w_o_sgu[l].astype(BF16),
            w_out[l].astype(BF16),
        )
        w1 = w_ffn_in[l].astype(BF16)
        w2 = w_ffn_out[l].astype(BF16)
        g1, g2 = row(norm1_g[l]), row(norm2_g[l])
        mod_lat = ada[l, 0:bsz]
        mod_ctx = jnp.broadcast_to(ada[l, bsz:bsz + 1], (bsz, 8, d))

        qc, kc, vc, lgc, obc, s_b = _gla_bwd(h_ctx, mod_ctx, g1, w_qkva, aup, ab, zero_state, n_tok=ctx_len)
        onc, s_f = _gla_fwd(qc, kc, vc, lgc, obc, zero_state, n_tok=ctx_len)
        if l < depth - 1:
            h_ctx = _mixer(h_ctx, onc, mod_ctx, g1, mix_w, n_tok=ctx_len, seg=ctx_len)
            h_ctx = _ffn(h_ctx, mod_ctx, g2, w1, w2, n_tok=ctx_len)

        q, k, v, lgf, ob, _ = _gla_bwd(h_lat, mod_lat, g1, w_qkva, aup, ab, s_b, n_tok=t_scan)
        on, _ = _gla_fwd(q, k, v, lgf, ob, s_f, n_tok=t_scan)
        h_lat = _mixer(h_lat, on, mod_lat, g1, mix_w, n_tok=t_mix, seg=GRID_W)
        h_lat = _ffn(h_lat, mod_lat, g2, w1, w2, row(final_g) if l == depth - 1 else None, n_tok=t_lat)
    return h_lat
```

```python
import functools

import jax
import jax.numpy as jnp
from jax import lax
from jax.experimental import pallas as pl
from jax.experimental.pallas import tpu as pltpu

F32 = jnp.float32
BF16 = jnp.bfloat16

D_MODEL = 1024
GRID_W = 64
GLA_HEADS = 4
GLA_KEY = D_MODEL // 2
GLA_VAL = D_MODEL
GLA_DK = GLA_KEY // GLA_HEADS
GLA_DV = GLA_VAL // GLA_HEADS
GLA_RANK = 16
GLA_CHUNK = 64
GLA_GATE_NORM = 16.0
CONV_K = 31
CONV_PAD = CONV_K // 2
SGU_GROUPS = 8
SGU_CHUNK = 128
D_FF = ((8 * D_MODEL + 3 * 256 - 1) // (3 * 256)) * 256
EPS = 1e-6

LANES = 128
SUBLANES = 8
MXU_COLS = 256
QKV_COLS = 2 * GLA_KEY + GLA_VAL
ADN_PAD = LANES
CONV_HALO = 16
VMEM_LIMIT = 56 * 1024 * 1024


def _dot(a, b):
    return jnp.dot(a, b, preferred_element_type=F32)


def _dot_nt(a, b):
    return lax.dot_general(a, b, (((1,), (1,)), ((), ())), preferred_element_type=F32)


def _dot_tn(a, b):
    return lax.dot_general(a, b, (((0,), (0,)), ((), ())), preferred_element_type=F32)


def _rms_mod(h, g, shift, scale):
    y = h * lax.rsqrt(jnp.mean(h * h, axis=-1, keepdims=True) + EPS) * g
    return y * (1.0 + scale) + shift


def _layer_norm(x, g, b):
    xc = x - jnp.mean(x, axis=-1, keepdims=True)
    y = xc * lax.rsqrt(jnp.mean(xc * xc, axis=-1, keepdims=True) + EPS)
    return y * g + b


def _sigmoid(x):
    return 0.5 * jnp.tanh(0.5 * x) + 0.5


def _silu(x):
    return x * _sigmoid(x)


def _log_sigmoid(z):
    return jnp.minimum(z, 0.0) - jnp.log(1.0 + jnp.exp(-jnp.abs(z)))


def _ada_kernel(c_ref, w_ref, b_ref, o_ref):
    o_ref[...] = _dot(_silu(c_ref[...]), w_ref[...]) + b_ref[...]


def _ada(cond, w_ada, b_ada):
    depth, d, n = w_ada.shape
    bn = n // 4
    return pl.pallas_call(
        _ada_kernel,
        grid=(depth, n // bn),
        in_specs=[
            pl.BlockSpec((8, d), lambda l, j: (0, 0)),
            pl.BlockSpec((None, d, bn), lambda l, j: (l, 0, j)),
            pl.BlockSpec((None, 1, bn), lambda l, j: (l, 0, j)),
        ],
        out_specs=pl.BlockSpec((None, 8, bn), lambda l, j: (l, 0, j)),
        out_shape=jax.ShapeDtypeStruct((depth, 8, n), F32),
        compiler_params=pltpu.CompilerParams(vmem_limit_bytes=VMEM_LIMIT),
        name="ada",
    )(cond, w_ada, b_ada.reshape(depth, 1, n))


def _gla_phases(q, k, v, lg, s_ref, o_ref, *, reverse, n_tok):
    c = GLA_CHUNK
    n_chunks = n_tok // c
    row = lax.broadcasted_iota(jnp.int32, (c, c), 0)
    col = lax.broadcasted_iota(jnp.int32, (c, c), 1)
    if reverse:
        tri = (col >= row).astype(BF16)
        keep = col > row
    else:
        tri = (col <= row).astype(BF16)
        keep = col <= row
    order = range(n_chunks - 1, -1, -1) if reverse else range(n_chunks)
    chunk_rows = [slice(ci * c, (ci + 1) * c) for ci in range(n_chunks)]
    ksls = [slice(hh * GLA_DK, (hh + 1) * GLA_DK) for hh in range(GLA_HEADS)]
    vsls = [slice(hh * GLA_DV, (hh + 1) * GLA_DV) for hh in range(GLA_HEADS)]
    heads = list(zip(ksls, vsls))
    val = {}

    def prep():
        lg_hi = lg.astype(BF16)
        lg_lo = (lg - lg_hi.astype(F32)).astype(BF16)
        cum = _dot(tri, jnp.concatenate([lg_hi[r] for r in chunk_rows] + [lg_lo[r] for r in chunk_rows], axis=1))
        qe, km, kd, dec = [], [], [], []
        for ci, rows in enumerate(chunk_rows):
            b = (cum[:, ci * GLA_KEY:(ci + 1) * GLA_KEY]
                 + cum[:, (n_chunks + ci) * GLA_KEY:(n_chunks + ci + 1) * GLA_KEY])
            b_last = b[0:1] if reverse else b[c - 1:c]
            kc = k[rows]
            qe.append((q[rows] * jnp.exp(b)).astype(BF16))
            km.append((kc * jnp.exp(-b)).astype(BF16))
            kd.append((kc * jnp.exp(b_last - b)).astype(BF16))
            dec.append(jnp.exp(b_last))
        val.update(qe=qe, km=km, kd=kd, dec=dec)

    def scores_and_increments():
        qe, km, kd = val["qe"], val["km"], val["kd"]
        val["scores"] = [[_dot_nt(qe[ci][:, ksl], km[ci][:, ksl]) for ksl, _ in heads] for ci in range(n_chunks)]
        val["upd"] = [[_dot_tn(v[rows, vsl], kd[ci][:, ksl]) for ksl, vsl in heads]
                      for ci, rows in enumerate(chunk_rows)]

    def intra():
        for ci, rows in enumerate(chunk_rows):
            for hh, (_, vsl) in enumerate(heads):
                a = jnp.where(keep, val["scores"][ci][hh], 0.0).astype(BF16)
                o_ref[rows, vsl] = _dot(a, v[rows, vsl])

    def recurrence():
        st = [s_ref[hh] for hh in range(GLA_HEADS)]
        s_in = [None] * n_chunks
        for ci in order:
            s_in[ci] = [s.astype(BF16) for s in st]
            st = [st[hh] * val["dec"][ci][:, ksls[hh]] + val["upd"][ci][hh] for hh in range(GLA_HEADS)]
        for hh in range(GLA_HEADS):
            s_ref[hh] = st[hh]
        val["s_in"] = s_in

    def inter():
        for ci, rows in enumerate(chunk_rows):
            for hh, (ksl, vsl) in enumerate(heads):
                o_ref[rows, vsl] += _dot_nt(val["qe"][ci][:, ksl], val["s_in"][ci][hh])

    return [prep, scores_and_increments, intra, recurrence, inter]


def _glu_to_halo(cin_scr, cpad, c, *, n_seg, seg):
    d = D_MODEL
    cols = slice(c * MXU_COLS, (c + 1) * MXU_COLS)
    hc = cin_scr[:, cols] * _sigmoid(cin_scr[:, d + c * MXU_COLS:d + (c + 1) * MXU_COLS])
    zeros = jnp.zeros((CONV_HALO, MXU_COLS), F32)
    for s in range(n_seg):
        cpad[s, 0:CONV_HALO, cols] = zeros
        cpad[s, CONV_HALO:CONV_HALO + seg, cols] = hc[s * seg:(s + 1) * seg]
        cpad[s, CONV_HALO + seg:2 * CONV_HALO + seg, cols] = zeros


def _conv_lane_tile(cpad, cw_ref, out_ref, s, l0, *, seg):
    first = CONV_HALO - CONV_PAD
    m = seg + SUBLANES
    acc = None
    for b in range(SUBLANES):
        yb = None
        for a in range((CONV_K + first + SUBLANES - 1) // SUBLANES):
            j = SUBLANES * a + b - first
            if 0 <= j < CONV_K:
                term = (cw_ref[j:j + 1, l0:l0 + LANES]
                        * cpad[s, SUBLANES * a:SUBLANES * a + m, l0:l0 + LANES])
                yb = term if yb is None else yb + term
        shifted = yb[b:b + seg]
        acc = shifted if acc is None else acc + shifted
    out_ref[s * seg:(s + 1) * seg, l0:l0 + LANES] = acc


def _gla_bwd_kernel(h_ref, mod_ref, g_ref, w_ref, aup_ref, ab_ref, s0_ref,
                    q_ref, k_ref, v_ref, lgf_ref, ob_ref, sfin_ref, s_scr, *, n_tok):
    t = pl.program_id(1)

    @pl.when(t == 0)
    def _():
        s_scr[...] = s0_ref[...]

    hm = _rms_mod(h_ref[...], g_ref[...], mod_ref[0:1, :], mod_ref[1:2, :]).astype(BF16)
    a_dn = _dot(hm, w_ref[:, QKV_COLS:QKV_COLS + ADN_PAD]).astype(BF16)
    z = _dot(a_dn, aup_ref[...]) + ab_ref[...]
    lg = _log_sigmoid(z) * (1.0 / GLA_GATE_NORM)
    p = _dot(hm, w_ref[:, 0:QKV_COLS])
    q = p[:, 0:GLA_KEY] * (GLA_DK ** -0.5)
    k = p[:, GLA_KEY:2 * GLA_KEY]
    v = p[:, 2 * GLA_KEY:QKV_COLS].astype(BF16)
    q_ref[...] = q.astype(BF16)
    k_ref[...] = k.astype(BF16)
    v_ref[...] = v
    lgf_ref[...] = lg[:, 0:GLA_KEY]
    for phase in _gla_phases(q, k, v, lg[:, GLA_KEY:2 * GLA_KEY], s_scr, ob_ref, reverse=True, n_tok=n_tok):
        phase()

    @pl.when(t == pl.num_programs(1) - 1)
    def _():
        sfin_ref[...] = s_scr[...]


def _gla_fwd_kernel(q_ref, k_ref, v_ref, lgf_ref, ob_ref, s0_ref, on_ref, sfin_ref, s_scr, o_scr, *, n_tok):
    t = pl.program_id(1)

    @pl.when(t == 0)
    def _():
        s_scr[...] = s0_ref[...]

    for phase in _gla_phases(q_ref[...].astype(F32), k_ref[...].astype(F32), v_ref[...], lgf_ref[...], s_scr, o_scr,
                             reverse=False, n_tok=n_tok):
        phase()
    for hh in range(GLA_HEADS):
        vsl = slice(hh * GLA_DV, (hh + 1) * GLA_DV)
        o = o_scr[:, vsl] + ob_ref[:, vsl]
        on_ref[:, vsl] = o * lax.rsqrt(jnp.mean(o * o, axis=-1, keepdims=True) + EPS)

    @pl.when(t == pl.num_programs(1) - 1)
    def _():
        sfin_ref[...] = s_scr[...]


def _const_spec(shape):
    nd = len(shape)
    return pl.BlockSpec(shape, lambda b, t: (0,) * nd, pipeline_mode=pl.Buffered(1))


def _state_spec():
    return pl.BlockSpec((None, GLA_HEADS, GLA_DV, GLA_DK), lambda b, t: (b, 0, 0, 0))


def _gla_bwd(h, mod, g, w_qkva, aup, ab, s0, *, n_tok):
    bsz, seq, d = h.shape
    n_t = seq // n_tok
    rev = lambda b, t: (b, n_t - 1 - t, 0)
    tok = lambda w: pl.BlockSpec((None, n_tok, w), rev)
    return pl.pallas_call(
        functools.partial(_gla_bwd_kernel, n_tok=n_tok),
        grid=(bsz, n_t),
        in_specs=[
            tok(d),
            pl.BlockSpec((None, 8, d), lambda b, t: (b, 0, 0)),
            _const_spec((1, d)),
            _const_spec(w_qkva.shape),
            _const_spec(aup.shape),
            _const_spec(ab.shape),
            _state_spec(),
        ],
        out_specs=[tok(GLA_KEY), tok(GLA_KEY), tok(GLA_VAL), tok(GLA_KEY), tok(GLA_VAL), _state_spec()],
        out_shape=[
            jax.ShapeDtypeStruct((bsz, seq, GLA_KEY), BF16),
            jax.ShapeDtypeStruct((bsz, seq, GLA_KEY), BF16),
            jax.ShapeDtypeStruct((bsz, seq, GLA_VAL), BF16),
            jax.ShapeDtypeStruct((bsz, seq, GLA_KEY), F32),
            jax.ShapeDtypeStruct((bsz, seq, GLA_VAL), F32),
            jax.ShapeDtypeStruct((bsz, GLA_HEADS, GLA_DV, GLA_DK), F32),
        ],
        scratch_shapes=[pltpu.VMEM((GLA_HEADS, GLA_DV, GLA_DK), F32)],
        compiler_params=pltpu.CompilerParams(
            dimension_semantics=("arbitrary", "arbitrary"), vmem_limit_bytes=VMEM_LIMIT,
            allow_input_fusion=[False, False, False, True, True, False, False]),
        name="gla_bwd",
    )(h, mod, g, w_qkva, aup, ab, s0)


def _gla_fwd(q, k, v, lgf, ob, s0, *, n_tok):
    bsz, seq, _ = v.shape
    n_t = seq // n_tok
    tok = lambda w: pl.BlockSpec((None, n_tok, w), lambda b, t: (b, t, 0))
    return pl.pallas_call(
        functools.partial(_gla_fwd_kernel, n_tok=n_tok),
        grid=(bsz, n_t),
        in_specs=[tok(GLA_KEY), tok(GLA_KEY), tok(GLA_VAL), tok(GLA_KEY), tok(GLA_VAL), _state_spec()],
        out_specs=[tok(GLA_VAL), _state_spec()],
        out_shape=[
            jax.ShapeDtypeStruct((bsz, seq, GLA_VAL), F32),
            jax.ShapeDtypeStruct((bsz, GLA_HEADS, GLA_DV, GLA_DK), F32),
        ],
        scratch_shapes=[pltpu.VMEM((GLA_HEADS, GLA_DV, GLA_DK), F32), pltpu.VMEM((n_tok, GLA_VAL), F32)],
        compiler_params=pltpu.CompilerParams(
            dimension_semantics=("arbitrary", "arbitrary"), vmem_limit_bytes=VMEM_LIMIT),
        name="gla_fwd",
    )(q, k, v, lgf, ob, s0)


class _Jobs:
    def __init__(self):
        self._jobs = []

    def add(self, unit, cost, fn, deps=()):
        self._jobs.append((unit, cost, fn, tuple(deps)))
        return len(self._jobs) - 1

    def emit(self):
        free = {"m": 0.0, "v": 0.0}
        start, finish = [], []
        for unit, cost, _, deps in self._jobs:
            t0 = max([free[unit]] + [finish[d] for d in deps])
            start.append(t0)
            finish.append(t0 + cost)
            free[unit] = t0 + cost
        for i in sorted(range(len(self._jobs)), key=lambda i: (start[i], i)):
            self._jobs[i][2]()


def _mixer_kernel(h_ref, on_ref, mod_ref, g_ref, w_r_ref, w_conv_ref, w_sgu_ref, w_gate_ref,
                  gla_g_ref, w_o_gla_ref, cw_ref, cb_ref, cln_g_ref, cln_b_ref, w_o_conv_ref,
                  sln_g_ref, sln_b_ref, ws_ref, sb_ref, w_o_sgu_ref, w_out_ref,
                  out_ref, cpad, cout, cin_scr, r_scr, sg_scr, gt_scr, sp_scr, ya_scr, yb_scr, yc_scr,
                  hm_scr, ain_scr, bin_scr, cin2_scr, yin_scr, *, n_tok, seg):
    d = D_MODEL
    n_seg = n_tok // seg
    n_col = d // MXU_COLS
    col = lambda c: slice(c * MXU_COLS, (c + 1) * MXU_COLS)
    mxu_cost = n_tok
    jobs = _Jobs()
    val = {}

    def f_hm():
        hm_scr[...] = _rms_mod(h_ref[...], g_ref[...], mod_ref[0:1, :], mod_ref[1:2, :]).astype(BF16)
    j_hm = jobs.add("v", 2.5 * n_tok, f_hm)

    def proj(w_ref, dst, c):
        def job():
            dst[:, col(c)] = _dot(hm_scr[...], w_ref[:, col(c)])
        return jobs.add("m", mxu_cost, job, (j_hm,))

    def out_chunk(src, w_ref, dst, c):
        def job():
            dst[:, col(c)] = _dot(src[...], w_ref[:, col(c)])
        return job

    gc = d // SGU_GROUPS
    row_chunks = [slice(n * SGU_CHUNK, (n + 1) * SGU_CHUNK) for n in range(n_tok // SGU_CHUNK)]

    def f_a():
        ain_scr[...] = (on_ref[...] * gla_g_ref[...] * _silu(r_scr[...])).astype(BF16)

    def f_sv(rows):
        def job():
            val["sv", rows.start] = _layer_norm(jax.nn.gelu(sg_scr[rows, d:2 * d]), sln_g_ref[...],
                                                sln_b_ref[...]).astype(BF16)
        return job

    def f_ws(rows):
        def job():
            sv = val["sv", rows.start]
            for gi in range(SGU_GROUPS):
                lsl = slice(gi * gc, (gi + 1) * gc)
                sp_scr[rows, lsl] = _dot(ws_ref[gi], sv[:, lsl]) + sb_ref[:, lsl]
        return job

    def f_c(rows):
        def job():
            cin2_scr[rows, :] = (jax.nn.gelu(sg_scr[rows, 0:d]) * sp_scr[rows, :]).astype(BF16)
        return job

    def f_b(rows):
        def job():
            bin_scr[rows, :] = _silu(_layer_norm(cout[rows, :] + cb_ref[...], cln_g_ref[...],
                                                 cln_b_ref[...])).astype(BF16)
        return job

    def f_m(c):
        def job():
            y = (_sigmoid(gt_scr[:, col(c)]) * ya_scr[:, col(c)]
                 + _sigmoid(gt_scr[:, d + c * MXU_COLS:d + (c + 1) * MXU_COLS]) * yb_scr[:, col(c)]
                 + _sigmoid(gt_scr[:, 2 * d + c * MXU_COLS:2 * d + (c + 1) * MXU_COLS]) * yc_scr[:, col(c)])
            yin_scr[:, col(c)] = y.astype(BF16)
        return job

    def f_o(c):
        def job():
            out_ref[:, col(c)] = h_ref[:, col(c)] + mod_ref[2:3, col(c)] * _dot(yin_scr[...], w_out_ref[:, col(c)])
        return job

    j_conv = []

    def conv_group(c):
        ja = proj(w_conv_ref, cin_scr, c)
        jb = proj(w_conv_ref, cin_scr, n_col + c)
        jg = jobs.add("v", 0.4 * n_tok, lambda: _glu_to_halo(cin_scr, cpad, c, n_seg=n_seg, seg=seg), (ja, jb))
        for l0 in range(c * MXU_COLS, (c + 1) * MXU_COLS, LANES):
            for s in range(n_seg):
                j_conv.append(jobs.add("v", 22 * (seg // SUBLANES),
                                       functools.partial(_conv_lane_tile, cpad, cw_ref, cout, s, l0, seg=seg), (jg,)))

    conv_group(0)
    j_r = [proj(w_r_ref, r_scr, c) for c in range(n_col)]
    j_a = jobs.add("v", 2 * n_tok, f_a, j_r)
    j_sg_hi = [proj(w_sgu_ref, sg_scr, c) for c in range(n_col, 2 * n_col)]
    j_sv = [jobs.add("v", 5 * SGU_CHUNK, f_sv(rows), j_sg_hi) for rows in row_chunks]
    conv_group(1)
    j_ya = [jobs.add("m", mxu_cost, out_chunk(ain_scr, w_o_gla_ref, ya_scr, c), (j_a,)) for c in range(n_col)]
    j_sg_lo = [proj(w_sgu_ref, sg_scr, c) for c in range(n_col)]
    j_ws = [jobs.add("m", 4 * SGU_CHUNK, f_ws(rows), (j,)) for rows, j in zip(row_chunks, j_sv)]
    j_cin = [jobs.add("v", 3.5 * SGU_CHUNK, f_c(rows), [j] + j_sg_lo) for rows, j in zip(row_chunks, j_ws)]
    conv_group(2)
    j_gt = [proj(w_gate_ref, gt_scr, c) for c in range(3 * n_col // 2)]
    conv_group(3)
    j_gt += [proj(w_gate_ref, gt_scr, c) for c in range(3 * n_col // 2, 3 * n_col)]
    j_yc = [jobs.add("m", mxu_cost, out_chunk(cin2_scr, w_o_sgu_ref, yc_scr, c), j_cin) for c in range(n_col)]
    j_bin = [jobs.add("v", 4 * SGU_CHUNK, f_b(rows), j_conv) for rows in row_chunks]
    j_yb = [jobs.add("m", mxu_cost, out_chunk(bin_scr, w_o_conv_ref, yb_scr, c), j_bin) for c in range(n_col)]
    j_mg = [jobs.add("v", 1.2 * n_tok, f_m(c),
                     (j_gt[c], j_gt[n_col + c], j_gt[2 * n_col + c], j_ya[c], j_yb[c], j_yc[c]))
            for c in range(n_col)]
    for c in range(n_col):
        jobs.add("m", mxu_cost, f_o(c), j_mg)

    jobs.emit()


def _mixer(h, on, mod, g, wts, *, n_tok, seg):
    bsz, seq, d = h.shape
    n_t = seq // n_tok
    tok = pl.BlockSpec((None, n_tok, d), lambda b, t: (b, t, 0))
    f32s = lambda w: pltpu.VMEM((n_tok, w), F32)
    bf16s = lambda w: pltpu.VMEM((n_tok, w), BF16)
    return pl.pallas_call(
        functools.partial(_mixer_kernel, n_tok=n_tok, seg=seg),
        grid=(bsz, n_t),
        in_specs=[tok, tok, pl.BlockSpec((None, 8, d), lambda b, t: (b, 0, 0)), _const_spec((1, d))]
        + [_const_spec(w.shape) for w in wts],
        out_specs=tok,
        out_shape=jax.ShapeDtypeStruct((bsz, seq, d), F32),
        scratch_shapes=[
            pltpu.VMEM((n_tok // seg, seg + 2 * CONV_HALO, d), F32),
            f32s(d), f32s(2 * d), f32s(d), f32s(2 * d), f32s(3 * d), f32s(d),
            f32s(d), f32s(d), f32s(d),
            bf16s(d), bf16s(d), bf16s(d), bf16s(d), bf16s(d),
        ],
        compiler_params=pltpu.CompilerParams(
            dimension_semantics=("parallel", "parallel"), vmem_limit_bytes=VMEM_LIMIT,
            allow_input_fusion=[False] * 4 + [True] * len(wts)),
        name="mixer",
    )(h, on, mod, g, *wts)


def _ffn_kernel(*refs, final):
    if final:
        h_ref, mod_ref, g_ref, w1_ref, w2_ref, fg_ref, out_ref = refs
    else:
        h_ref, mod_ref, g_ref, w1_ref, w2_ref, out_ref = refs
    h = h_ref[...]
    hm = _rms_mod(h, g_ref[...], mod_ref[3:4, :], mod_ref[4:5, :]).astype(BF16)
    gu = _dot(hm, w1_ref[...])
    act = (_silu(gu[:, 0:D_FF]) * gu[:, D_FF:2 * D_FF]).astype(BF16)
    out = h + mod_ref[5:6, :] * _dot(act, w2_ref[...])
    if final:
        out = out * lax.rsqrt(jnp.mean(out * out, axis=-1, keepdims=True) + EPS) * fg_ref[...]
    out_ref[...] = out


def _ffn(h, mod, g, w1, w2, final_g=None, *, n_tok):
    bsz, seq, d = h.shape
    tok = pl.BlockSpec((None, n_tok, d), lambda b, t: (b, t, 0))
    final = final_g is not None
    extra = (final_g,) if final else ()
    return pl.pallas_call(
        functools.partial(_ffn_kernel, final=final),
        grid=(bsz, seq // n_tok),
        in_specs=[tok, pl.BlockSpec((None, 8, d), lambda b, t: (b, 0, 0)), _const_spec((1, d)),
                  _const_spec(w1.shape), _const_spec(w2.shape)] + [_const_spec((1, d)) for _ in extra],
        out_specs=tok,
        out_shape=jax.ShapeDtypeStruct((bsz, seq, d), F32),
        compiler_params=pltpu.CompilerParams(
            dimension_semantics=("parallel", "parallel"), vmem_limit_bytes=VMEM_LIMIT,
            allow_input_fusion=[False] * 3 + [True] * 2 + [False] * len(extra)),
        name="ffn",
    )(h, mod, g, w1, w2, *extra)


def _pick_tile(seq, want):
    return min(seq, want)


def kernel(x, c, ctx, c_ctx, w_ada, b_ada, norm1_g, norm2_g, w_in, gla_a_up, gla_a_b, gla_norm_g, w_o_gla,
           conv_w, conv_b, conv_ln_g, conv_ln_b, w_o_conv, sgu_ln_g, sgu_ln_b, sgu_ws, sgu_b, w_o_sgu, w_out,
           w_ffn_in, w_ffn_out, final_g):
    bsz, seq, d = x.shape
    depth = w_ada.shape[0]
    ctx_len = ctx.shape[1]

    cond = jnp.zeros((8, d), F32).at[0:bsz].set(c).at[bsz].set(c_ctx)
    ada = _ada(cond, w_ada, b_ada).reshape(depth, 8, 6, d)
    ada = jnp.pad(ada, ((0, 0), (0, 0), (0, 2), (0, 0)))

    row = lambda a: a.reshape(1, -1)
    c0 = 0
    offs = []
    for wdt in (GLA_KEY, GLA_KEY, GLA_VAL, 2 * GLA_RANK, GLA_VAL, 2 * D_MODEL, 2 * D_MODEL, 3 * D_MODEL):
        offs.append((c0, c0 + wdt))
        c0 += wdt
    (_, _), (_, _), (_, v_end), (a_lo, a_hi), (r_lo, r_hi), (cv_lo, cv_hi), (sg_lo, sg_hi), (gt_lo, gt_hi) = offs

    h_lat, h_ctx = x, ctx
    zero_state = jnp.zeros((bsz, GLA_HEADS, GLA_DV, GLA_DK), F32)
    t_scan = _pick_tile(seq, 1024)
    t_lat = _pick_tile(seq, 512)
    t_mix = _pick_tile(seq, 256)
    for l in range(depth):
        wl = w_in[l]
        w_qkva = jnp.pad(wl[:, 0:a_hi], ((0, 0), (0, ADN_PAD - (a_hi - a_lo)))).astype(BF16)
        aup = jnp.zeros((ADN_PAD, 2 * GLA_KEY), F32)
        aup = aup.at[0:GLA_RANK, 0:GLA_KEY].set(gla_a_up[l, 0])
        aup = aup.at[GLA_RANK:2 * GLA_RANK, GLA_KEY:2 * GLA_KEY].set(gla_a_up[l, 1]).astype(BF16)
        ab = gla_a_b[l].reshape(1, 2 * GLA_KEY)
        sgu_bias = jnp.repeat(sgu_b[l].T, D_MODEL // SGU_GROUPS, axis=1)
        mix_w = (
            wl[:, r_lo:r_hi].astype(BF16), wl[:, cv_lo:cv_hi].astype(BF16), wl[:, sg_lo:sg_hi].astype(BF16),
            wl[:, gt_lo:gt_hi].astype(BF16),
            row(gla_norm_g[l]), w_o_gla[l].astype(BF16),
            jnp.pad(conv_w[l], ((0, 1), (0, 0))), row(conv_b[l]), row(conv_ln_g[l]), row(conv_ln_b[l]),
            w_o_conv[l].astype(BF16),
            row(sgu_ln_g[l]), row(sgu_ln_b[l]), sgu_ws[l].astype(BF16), sgu_bias, w_o_sgu[l].astype(BF16),
            w_out[l].astype(BF16),
        )
        w1 = w_ffn_in[l].astype(BF16)
        w2 = w_ffn_out[l].astype(BF16)
        g1, g2 = row(norm1_g[l]), row(norm2_g[l])
        mod_lat = ada[l, 0:bsz]
        mod_ctx = jnp.broadcast_to(ada[l, bsz:bsz + 1], (bsz, 8, d))

        qc, kc, vc, lgc, obc, s_b = _gla_bwd(h_ctx, mod_ctx, g1, w_qkva, aup, ab, zero_state, n_tok=ctx_len)
        onc, s_f = _gla_fwd(qc, kc, vc, lgc, obc, zero_state, n_tok=ctx_len)
        if l < depth - 1:
            h_ctx = _mixer(h_ctx, onc, mod_ctx, g1, mix_w, n_tok=ctx_len, seg=ctx_len)
            h_ctx = _ffn(h_ctx, mod_ctx, g2, w1, w2, n_tok=ctx_len)

        q, k, v, lgf, ob, _ = _gla_bwd(h_lat, mod_lat, g1, w_qkva, aup, ab, s_b, n_tok=t_scan)
        on, _ = _gla_fwd(q, k, v, lgf, ob, s_f, n_tok=t_scan)
        h_lat = _mixer(h_lat, on, mod_lat, g1, mix_w, n_tok=t_mix, seg=GRID_W)
        h_lat = _ffn(h_lat, mod_lat, g2, w1, w2, row(final_g) if l == depth - 1 else None, n_tok=t_lat)
    return h_lat
```

```python
import functools

import jax
import jax.numpy as jnp
from jax import lax
from jax.experimental import pallas as pl
from jax.experimental.pallas import tpu as pltpu

F32 = jnp.float32
BF16 = jnp.bfloat16

D_MODEL = 1024
GRID_W = 64
GLA_HEADS = 4
GLA_KEY = D_MODEL // 2
GLA_VAL = D_MODEL
GLA_DK = GLA_KEY // GLA_HEADS
GLA_DV = GLA_VAL // GLA_HEADS
GLA_RANK = 16
GLA_CHUNK = 64
GLA_GATE_NORM = 16.0
CONV_K = 31
CONV_PAD = CONV_K // 2
SGU_GROUPS = 8
SGU_CHUNK = 128
D_FF = ((8 * D_MODEL + 3 * 256 - 1) // (3 * 256)) * 256
EPS = 1e-6

LANES = 128
SUBLANES = 8
MXU_COLS = 256
QKV_COLS = 2 * GLA_KEY + GLA_VAL
ADN_PAD = LANES
CONV_HALO = 16
VMEM_LIMIT = 56 * 1024 * 1024


def _dot(a, b):
    return jnp.dot(a, b, preferred_element_type=F32)


def _dot_nt(a, b):
    return lax.dot_general(a, b, (((1,), (1,)), ((), ())), preferred_element_type=F32)


def _dot_tn(a, b):
    return lax.dot_general(a, b, (((0,), (0,)), ((), ())), preferred_element_type=F32)


def _rms_mod(h, g, shift, scale):
    y = h * lax.rsqrt(jnp.mean(h * h, axis=-1, keepdims=True) + EPS) * g
    return y * (1.0 + scale) + shift


def _layer_norm(x, g, b):
    xc = x - jnp.mean(x, axis=-1, keepdims=True)
    y = xc * lax.rsqrt(jnp.mean(xc * xc, axis=-1, keepdims=True) + EPS)
    return y * g + b


def _sigmoid(x):
    return 0.5 * jnp.tanh(0.5 * x) + 0.5


def _silu(x):
    return x * _sigmoid(x)


def _log_sigmoid(z):
    return jnp.minimum(z, 0.0) - jnp.log(1.0 + jnp.exp(-jnp.abs(z)))


def _ada_kernel(c_ref, w_ref, b_ref, o_ref):
    o_ref[...] = _dot(_silu(c_ref[...]), w_ref[...]) + b_ref[...]


def _ada(cond, w_ada, b_ada):
    depth, d, n = w_ada.shape
    bn = n // 4
    return pl.pallas_call(
        _ada_kernel,
        grid=(depth, n // bn),
        in_specs=[
            pl.BlockSpec((8, d), lambda l, j: (0, 0)),
            pl.BlockSpec((None, d, bn), lambda l, j: (l, 0, j)),
            pl.BlockSpec((None, 1, bn), lambda l, j: (l, 0, j)),
        ],
        out_specs=pl.BlockSpec((None, 8, bn), lambda l, j: (l, 0, j)),
        out_shape=jax.ShapeDtypeStruct((depth, 8, n), F32),
        compiler_params=pltpu.CompilerParams(vmem_limit_bytes=VMEM_LIMIT),
        name="ada",
    )(cond, w_ada, b_ada.reshape(depth, 1, n))


def _gla_phases(q, k, v, lg, s_ref, o_ref, *, reverse, n_tok):
    c = GLA_CHUNK
    n_chunks = n_tok // c
    row = lax.broadcasted_iota(jnp.int32, (c, c), 0)
    col = lax.broadcasted_iota(jnp.int32, (c, c), 1)
    if reverse:
        tri = (col >= row).astype(BF16)
        keep = col > row
    else:
        tri = (col <= row).astype(BF16)
        keep = col <= row
    order = range(n_chunks - 1, -1, -1) if reverse else range(n_chunks)
    chunk_rows = [slice(ci * c, (ci + 1) * c) for ci in range(n_chunks)]
    ksls = [slice(hh * GLA_DK, (hh + 1) * GLA_DK) for hh in range(GLA_HEADS)]
    vsls = [slice(hh * GLA_DV, (hh + 1) * GLA_DV) for hh in range(GLA_HEADS)]
    heads = list(zip(ksls, vsls))
    val = {}

    def prep():
        lg_hi = lg.astype(BF16)
        lg_lo = (lg - lg_hi.astype(F32)).astype(BF16)
        cum = _dot(tri, jnp.concatenate([lg_hi[r] for r in chunk_rows] + [lg_lo[r] for r in chunk_rows], axis=1))
        qe, km, kd, dec = [], [], [], []
        for ci, rows in enumerate(chunk_rows):
            b = (cum[:, ci * GLA_KEY:(ci + 1) * GLA_KEY]
                 + cum[:, (n_chunks + ci) * GLA_KEY:(n_chunks + ci + 1) * GLA_KEY])
            b_last = b[0:1] if reverse else b[c - 1:c]
            kc = k[rows]
            qe.append((q[rows] * jnp.exp(b)).astype(BF16))
            km.append((kc * jnp.exp(-b)).astype(BF16))
            kd.append((kc * jnp.exp(b_last - b)).astype(BF16))
            dec.append(jnp.exp(b_last))
        val.update(qe=qe, km=km, kd=kd, dec=dec)

    def scores_and_increments():
        qe, km, kd = val["qe"], val["km"], val["kd"]
        val["scores"] = [[_dot_nt(qe[ci][:, ksl], km[ci][:, ksl]) for ksl, _ in heads] for ci in range(n_chunks)]
        val["upd"] = [[_dot_tn(v[rows, vsl], kd[ci][:, ksl]) for ksl, vsl in heads]
                      for ci, rows in enumerate(chunk_rows)]

    def intra():
        for ci, rows in enumerate(chunk_rows):
            for hh, (_, vsl) in enumerate(heads):
                a = jnp.where(keep, val["scores"][ci][hh], 0.0).astype(BF16)
                o_ref[rows, vsl] = _dot(a, v[rows, vsl])

    def recurrence():
        st = [s_ref[hh] for hh in range(GLA_HEADS)]
        s_in = [None] * n_chunks
        for ci in order:
            s_in[ci] = [s.astype(BF16) for s in st]
            st = [st[hh] * val["dec"][ci][:, ksls[hh]] + val["upd"][ci][hh] for hh in range(GLA_HEADS)]
        for hh in range(GLA_HEADS):
            s_ref[hh] = st[hh]
        val["s_in"] = s_in

    def inter():
        for ci, rows in enumerate(chunk_rows):
            for hh, (ksl, vsl) in enumerate(heads):
                o_ref[rows, vsl] += _dot_nt(val["qe"][ci][:, ksl], val["s_in"][ci][hh])

    return [prep, scores_and_increments, intra, recurrence, inter]


def _glu_to_halo(cin_scr, cpad, c, *, n_seg, seg):
    d = D_MODEL
    cols = slice(c * MXU_COLS, (c + 1) * MXU_COLS)
    hc = cin_scr[:, cols] * _sigmoid(cin_scr[:, d + c * MXU_COLS:d + (c + 1) * MXU_COLS])
    zeros = jnp.zeros((CONV_HALO, MXU_COLS), F32)
    for s in range(n_seg):
        cpad[s, 0:CONV_HALO, cols] = zeros
        cpad[s, CONV_HALO:CONV_HALO + seg, cols] = hc[s * seg:(s + 1) * seg]
        cpad[s, CONV_HALO + seg:2 * CONV_HALO + seg, cols] = zeros


def _conv_lane_tile(cpad, cw_ref, out_ref, s, l0, *, seg):
    first = CONV_HALO - CONV_PAD
    m = seg + SUBLANES
    acc = None
    for b in range(SUBLANES):
        yb = None
        for a in range((CONV_K + first + SUBLANES - 1) // SUBLANES):
            j = SUBLANES * a + b - first
            if 0 <= j < CONV_K:
                term = (cw_ref[j:j + 1, l0:l0 + LANES]
                        * cpad[s, SUBLANES * a:SUBLANES * a + m, l0:l0 + LANES])
                yb = term if yb is None else yb + term
        shifted = yb[b:b + seg]
        acc = shifted if acc is None else acc + shifted
    out_ref[s * seg:(s + 1) * seg, l0:l0 + LANES] = acc


def _gla_bwd_kernel(h_ref, mod_ref, g_ref, w_ref, aup_ref, ab_ref, s0_ref,
                    q_ref, k_ref, v_ref, lgf_ref, ob_ref, sfin_ref, s_scr, *, n_tok):
    t = pl.program_id(1)

    @pl.when(t == 0)
    def _():
        s_scr[...] = s0_ref[...]

    hm = _rms_mod(h_ref[...], g_ref[...], mod_ref[0:1, :], mod_ref[1:2, :]).astype(BF16)
    a_dn = _dot(hm, w_ref[:, QKV_COLS:QKV_COLS + ADN_PAD]).astype(BF16)
    z = _dot(a_dn, aup_ref[...]) + ab_ref[...]
    lg = _log_sigmoid(z) * (1.0 / GLA_GATE_NORM)
    p = _dot(hm, w_ref[:, 0:QKV_COLS])
    q = p[:, 0:GLA_KEY] * (GLA_DK ** -0.5)
    k = p[:, GLA_KEY:2 * GLA_KEY]
    v = p[:, 2 * GLA_KEY:QKV_COLS].astype(BF16)
    q_ref[...] = q.astype(BF16)
    k_ref[...] = k.astype(BF16)
    v_ref[...] = v
    lgf_ref[...] = lg[:, 0:GLA_KEY]
    for phase in _gla_phases(q, k, v, lg[:, GLA_KEY:2 * GLA_KEY], s_scr, ob_ref, reverse=True, n_tok=n_tok):
        phase()

    @pl.when(t == pl.num_programs(1) - 1)
    def _():
        sfin_ref[...] = s_scr[...]


def _gla_fwd_kernel(q_ref, k_ref, v_ref, lgf_ref, ob_ref, s0_ref, on_ref, sfin_ref, s_scr, o_scr, *, n_tok):
    t = pl.program_id(1)

    @pl.when(t == 0)
    def _():
        s_scr[...] = s0_ref[...]

    for phase in _gla_phases(q_ref[...].astype(F32), k_ref[...].astype(F32), v_ref[...], lgf_ref[...], s_scr, o_scr,
                             reverse=False, n_tok=n_tok):
        phase()
    for hh in range(GLA_HEADS):
        vsl = slice(hh * GLA_DV, (hh + 1) * GLA_DV)
        o = o_scr[:, vsl] + ob_ref[:, vsl]
        on_ref[:, vsl] = o * lax.rsqrt(jnp.mean(o * o, axis=-1, keepdims=True) + EPS)

    @pl.when(t == pl.num_programs(1) - 1)
    def _():
        sfin_ref[...] = s_scr[...]


def _const_spec(shape):
    nd = len(shape)
    return pl.BlockSpec(shape, lambda b, t: (0,) * nd, pipeline_mode=pl.Buffered(1))


def _state_spec():
    return pl.BlockSpec((None, GLA_HEADS, GLA_DV, GLA_DK), lambda b, t: (b, 0, 0, 0))


def _gla_bwd(h, mod, g, w_qkva, aup, ab, s0, *, n_tok):
    bsz, seq, d = h.shape
    n_t = seq // n_tok
    rev = lambda b, t: (b, n_t - 1 - t, 0)
    tok = lambda w: pl.BlockSpec((None, n_tok, w), rev)
    return pl.pallas_call(
        functools.partial(_gla_bwd_kernel, n_tok=n_tok),
        grid=(bsz, n_t),
        in_specs=[
            tok(d),
            pl.BlockSpec((None, 8, d), lambda b, t: (b, 0, 0)),
            _const_spec((1, d)),
            _const_spec(w_qkva.shape),
            _const_spec(aup.shape),
            _const_spec(ab.shape),
            _state_spec(),
        ],
        out_specs=[tok(GLA_KEY), tok(GLA_KEY), tok(GLA_VAL), tok(GLA_KEY), tok(GLA_VAL), _state_spec()],
        out_shape=[
            jax.ShapeDtypeStruct((bsz, seq, GLA_KEY), BF16),
            jax.ShapeDtypeStruct((bsz, seq, GLA_KEY), BF16),
            jax.ShapeDtypeStruct((bsz, seq, GLA_VAL), BF16),
            jax.ShapeDtypeStruct((bsz, seq, GLA_KEY), F32),
            jax.ShapeDtypeStruct((bsz, seq, GLA_VAL), F32),
            jax.ShapeDtypeStruct((bsz, GLA_HEADS, GLA_DV, GLA_DK), F32),
        ],
        scratch_shapes=[pltpu.VMEM((GLA_HEADS, GLA_DV, GLA_DK), F32)],
        compiler_params=pltpu.CompilerParams(
            dimension_semantics=("arbitrary", "arbitrary"), vmem_limit_bytes=VMEM_LIMIT),
        name="gla_bwd",
    )(h, mod, g, w_qkva, aup, ab, s0)


def _gla_fwd(q, k, v, lgf, ob, s0, *, n_tok):
    bsz, seq, _ = v.shape
    n_t = seq // n_tok
    tok = lambda w: pl.BlockSpec((None, n_tok, w), lambda b, t: (b, t, 0))
    return pl.pallas_call(
        functools.partial(_gla_fwd_kernel, n_tok=n_tok),
        grid=(bsz, n_t),
        in_specs=[tok(GLA_KEY), tok(GLA_KEY), tok(GLA_VAL), tok(GLA_KEY), tok(GLA_VAL), _state_spec()],
        out_specs=[tok(GLA_VAL), _state_spec()],
        out_shape=[
            jax.ShapeDtypeStruct((bsz, seq, GLA_VAL), F32),
            jax.ShapeDtypeStruct((bsz, GLA_HEADS, GLA_DV, GLA_DK), F32),
        ],
        scratch_shapes=[pltpu.VMEM((GLA_HEADS, GLA_DV, GLA_DK), F32), pltpu.VMEM((n_tok, GLA_VAL), F32)],
        compiler_params=pltpu.CompilerParams(
            dimension_semantics=("arbitrary", "arbitrary"), vmem_limit_bytes=VMEM_LIMIT),
        name="gla_fwd",
    )(q, k, v, lgf, ob, s0)


class _Jobs:
    def __init__(self):
        self._jobs = []

    def add(self, unit, cost, fn, deps=()):
        self._jobs.append((unit, cost, fn, tuple(deps)))
        return len(self._jobs) - 1

    def emit(self):
        free = {"m": 0.0, "v": 0.0}
        start, finish = [], []
        for unit, cost, _, deps in self._jobs:
            t0 = max([free[unit]] + [finish[d] for d in deps])
            start.append(t0)
            finish.append(t0 + cost)
            free[unit] = t0 + cost
        for i in sorted(range(len(self._jobs)), key=lambda i: (start[i], i)):
            self._jobs[i][2]()


def _mixer_kernel(h_ref, on_ref, mod_ref, g_ref, w_r_ref, w_conv_ref, w_sgu_ref, w_gate_ref,
                  gla_g_ref, w_o_gla_ref, cw_ref, cb_ref, cln_g_ref, cln_b_ref, w_o_conv_ref,
                  sln_g_ref, sln_b_ref, ws_ref, sb_ref, w_o_sgu_ref, w_out_ref,
                  out_ref, cpad, cout, cin_scr, r_scr, sg_scr, gt_scr, sp_scr, ya_scr, yb_scr, yc_scr,
                  hm_scr, ain_scr, bin_scr, cin2_scr, yin_scr, *, n_tok, seg):
    d = D_MODEL
    n_seg = n_tok // seg
    n_col = d // MXU_COLS
    col = lambda c: slice(c * MXU_COLS, (c + 1) * MXU_COLS)
    mxu_cost = n_tok
    jobs = _Jobs()
    val = {}

    def f_hm():
        hm_scr[...] = _rms_mod(h_ref[...], g_ref[...], mod_ref[0:1, :], mod_ref[1:2, :]).astype(BF16)
    j_hm = jobs.add("v", 2.5 * n_tok, f_hm)

    def proj(w_ref, dst, c):
        def job():
            dst[:, col(c)] = _dot(hm_scr[...], w_ref[:, col(c)])
        return jobs.add("m", mxu_cost, job, (j_hm,))

    def out_chunk(src, w_ref, dst, c):
        def job():
            dst[:, col(c)] = _dot(src[...], w_ref[:, col(c)])
        return job

    gc = d // SGU_GROUPS
    row_chunks = [slice(n * SGU_CHUNK, (n + 1) * SGU_CHUNK) for n in range(n_tok // SGU_CHUNK)]

    def f_a():
        ain_scr[...] = (on_ref[...] * gla_g_ref[...] * _silu(r_scr[...])).astype(BF16)

    def f_sv(rows):
        def job():
            val["sv", rows.start] = _layer_norm(jax.nn.gelu(sg_scr[rows, d:2 * d]), sln_g_ref[...],
                                                sln_b_ref[...]).astype(BF16)
        return job

    def f_ws():
        svs = [val["sv", rows.start] for rows in row_chunks]
        for gi in range(SGU_GROUPS):
            lsl = slice(gi * gc, (gi + 1) * gc)
            sp = _dot(ws_ref[gi], jnp.concatenate([sv[:, lsl] for sv in svs], axis=1))
            for n, rows in enumerate(row_chunks):
                sp_scr[rows, lsl] = sp[:, n * gc:(n + 1) * gc] + sb_ref[:, lsl]

    def f_c(rows):
        def job():
            cin2_scr[rows, :] = (jax.nn.gelu(sg_scr[rows, 0:d]) * sp_scr[rows, :]).astype(BF16)
        return job

    def f_b(rows):
        def job():
            bin_scr[rows, :] = _silu(_layer_norm(cout[rows, :] + cb_ref[...], cln_g_ref[...],
                                                 cln_b_ref[...])).astype(BF16)
        return job

    def f_m(c):
        def job():
            y = (_sigmoid(gt_scr[:, col(c)]) * ya_scr[:, col(c)]
                 + _sigmoid(gt_scr[:, d + c * MXU_COLS:d + (c + 1) * MXU_COLS]) * yb_scr[:, col(c)]
                 + _sigmoid(gt_scr[:, 2 * d + c * MXU_COLS:2 * d + (c + 1) * MXU_COLS]) * yc_scr[:, col(c)])
            yin_scr[:, col(c)] = y.astype(BF16)
        return job

    def f_o(c):
        def job():
            out_ref[:, col(c)] = h_ref[:, col(c)] + mod_ref[2:3, col(c)] * _dot(yin_scr[...], w_out_ref[:, col(c)])
        return job

    j_conv = []

    def conv_group(c):
        ja = proj(w_conv_ref, cin_scr, c)
        jb = proj(w_conv_ref, cin_scr, n_col + c)
        jg = jobs.add("v", 0.4 * n_tok, lambda: _glu_to_halo(cin_scr, cpad, c, n_seg=n_seg, seg=seg), (ja, jb))
        for l0 in range(c * MXU_COLS, (c + 1) * MXU_COLS, LANES):
            for s in range(n_seg):
                j_conv.append(jobs.add("v", 22 * (seg // SUBLANES),
                                       functools.partial(_conv_lane_tile, cpad, cw_ref, cout, s, l0, seg=seg), (jg,)))

    conv_group(0)
    j_r = [proj(w_r_ref, r_scr, c) for c in range(n_col)]
    j_a = jobs.add("v", 2 * n_tok, f_a, j_r)
    j_sg_hi = [proj(w_sgu_ref, sg_scr, c) for c in range(n_col, 2 * n_col)]
    j_sv = [jobs.add("v", 5 * SGU_CHUNK, f_sv(rows), j_sg_hi) for rows in row_chunks]
    conv_group(1)
    j_ya = [jobs.add("m", mxu_cost, out_chunk(ain_scr, w_o_gla_ref, ya_scr, c), (j_a,)) for c in range(n_col)]
    j_sg_lo = [proj(w_sgu_ref, sg_scr, c) for c in range(n_col)]
    j_ws = [jobs.add("m", 2 * SGU_CHUNK * len(row_chunks), f_ws, j_sv)] * len(row_chunks)
    j_cin = [jobs.add("v", 3.5 * SGU_CHUNK, f_c(rows), [j] + j_sg_lo) for rows, j in zip(row_chunks, j_ws)]
    conv_group(2)
    j_gt = [proj(w_gate_ref, gt_scr, c) for c in range(3 * n_col // 2)]
    conv_group(3)
    j_gt += [proj(w_gate_ref, gt_scr, c) for c in range(3 * n_col // 2, 3 * n_col)]
    j_yc = [jobs.add("m", mxu_cost, out_chunk(cin2_scr, w_o_sgu_ref, yc_scr, c), j_cin) for c in range(n_col)]
    j_bin = [jobs.add("v", 4 * SGU_CHUNK, f_b(rows), j_conv) for rows in row_chunks]
    j_yb = [jobs.add("m", mxu_cost, out_chunk(bin_scr, w_o_conv_ref, yb_scr, c), j_bin) for c in range(n_col)]
    j_mg = [jobs.add("v", 1.2 * n_tok, f_m(c),
                     (j_gt[c], j_gt[n_col + c], j_gt[2 * n_col + c], j_ya[c], j_yb[c], j_yc[c]))
            for c in range(n_col)]
    for c in range(n_col):
        jobs.add("m", mxu_cost, f_o(c), j_mg)

    jobs.emit()


def _mixer(h, on, mod, g, wts, *, n_tok, seg):
    bsz, seq, d = h.shape
    n_t = seq // n_tok
    tok = pl.BlockSpec((None, n_tok, d), lambda b, t: (b, t, 0))
    f32s = lambda w: pltpu.VMEM((n_tok, w), F32)
    bf16s = lambda w: pltpu.VMEM((n_tok, w), BF16)
    return pl.pallas_call(
        functools.partial(_mixer_kernel, n_tok=n_tok, seg=seg),
        grid=(bsz, n_t),
        in_specs=[tok, tok, pl.BlockSpec((None, 8, d), lambda b, t: (b, 0, 0)), _const_spec((1, d))]
        + [_const_spec(w.shape) for w in wts],
        out_specs=tok,
        out_shape=jax.ShapeDtypeStruct((bsz, seq, d), F32),
        scratch_shapes=[
            pltpu.VMEM((n_tok // seg, seg + 2 * CONV_HALO, d), F32),
            f32s(d), f32s(2 * d), f32s(d), f32s(2 * d), f32s(3 * d), f32s(d),
            f32s(d), f32s(d), f32s(d),
            bf16s(d), bf16s(d), bf16s(d), bf16s(d), bf16s(d),
        ],
        compiler_params=pltpu.CompilerParams(
            dimension_semantics=("parallel", "parallel"), vmem_limit_bytes=VMEM_LIMIT),
        name="mixer",
    )(h, on, mod, g, *wts)


def _ffn_kernel(*refs, final):
    if final:
        h_ref, mod_ref, g_ref, w1_ref, w2_ref, fg_ref, out_ref = refs
    else:
        h_ref, mod_ref, g_ref, w1_ref, w2_ref, out_ref = refs
    h = h_ref[...]
    hm = _rms_mod(h, g_ref[...], mod_ref[3:4, :], mod_ref[4:5, :]).astype(BF16)
    gu = _dot(hm, w1_ref[...])
    act = (_silu(gu[:, 0:D_FF]) * gu[:, D_FF:2 * D_FF]).astype(BF16)
    out = h + mod_ref[5:6, :] * _dot(act, w2_ref[...])
    if final:
        out = out * lax.rsqrt(jnp.mean(out * out, axis=-1, keepdims=True) + EPS) * fg_ref[...]
    out_ref[...] = out


def _ffn(h, mod, g, w1, w2, final_g=None, *, n_tok):
    bsz, seq, d = h.shape
    tok = pl.BlockSpec((None, n_tok, d), lambda b, t: (b, t, 0))
    final = final_g is not None
    extra = (final_g,) if final else ()
    return pl.pallas_call(
        functools.partial(_ffn_kernel, final=final),
        grid=(bsz, seq // n_tok),
        in_specs=[tok, pl.BlockSpec((None, 8, d), lambda b, t: (b, 0, 0)), _const_spec((1, d)),
                  _const_spec(w1.shape), _const_spec(w2.shape)] + [_const_spec((1, d)) for _ in extra],
        out_specs=tok,
        out_shape=jax.ShapeDtypeStruct((bsz, seq, d), F32),
        compiler_params=pltpu.CompilerParams(
            dimension_semantics=("parallel", "parallel"), vmem_limit_bytes=VMEM_LIMIT),
        name="ffn",
    )(h, mod, g, w1, w2, *extra)


def _pick_tile(seq, want):
    return min(seq, want)


def kernel(x, c, ctx, c_ctx, w_ada, b_ada, norm1_g, norm2_g, w_in, gla_a_up, gla_a_b, gla_norm_g, w_o_gla,
           conv_w, conv_b, conv_ln_g, conv_ln_b, w_o_conv, sgu_ln_g, sgu_ln_b, sgu_ws, sgu_b, w_o_sgu, w_out,
           w_ffn_in, w_ffn_out, final_g):
    bsz, seq, d = x.shape
    depth = w_ada.shape[0]
    ctx_len = ctx.shape[1]

    cond = jnp.zeros((8, d), F32).at[0:bsz].set(c).at[bsz].set(c_ctx)
    ada = _ada(cond, w_ada, b_ada).reshape(depth, 8, 6, d)
    ada = jnp.pad(ada, ((0, 0), (0, 0), (0, 2), (0, 0)))

    row = lambda a: a.reshape(1, -1)
    c0 = 0
    offs = []
    for wdt in (GLA_KEY, GLA_KEY, GLA_VAL, 2 * GLA_RANK, GLA_VAL, 2 * D_MODEL, 2 * D_MODEL, 3 * D_MODEL):
        offs.append((c0, c0 + wdt))
        c0 += wdt
    (_, _), (_, _), (_, v_end), (a_lo, a_hi), (r_lo, r_hi), (cv_lo, cv_hi), (sg_lo, sg_hi), (gt_lo, gt_hi) = offs

    h_lat, h_ctx = x, ctx
    zero_state = jnp.zeros((bsz, GLA_HEADS, GLA_DV, GLA_DK), F32)
    t_scan = _pick_tile(seq, 1024)
    t_lat = _pick_tile(seq, 512)
    t_mix = _pick_tile(seq, 256)
    for l in range(depth):
        wl = w_in[l]
        w_qkva = jnp.pad(wl[:, 0:a_hi], ((0, 0), (0, ADN_PAD - (a_hi - a_lo)))).astype(BF16)
        aup = jnp.zeros((ADN_PAD, 2 * GLA_KEY), F32)
        aup = aup.at[0:GLA_RANK, 0:GLA_KEY].set(gla_a_up[l, 0])
        aup = aup.at[GLA_RANK:2 * GLA_RANK, GLA_KEY:2 * GLA_KEY].set(gla_a_up[l, 1]).astype(BF16)
        ab = gla_a_b[l].reshape(1, 2 * GLA_KEY)
        sgu_bias = jnp.repeat(sgu_b[l].T, D_MODEL // SGU_GROUPS, axis=1)
        mix_w = (
            wl[:, r_lo:r_hi].astype(BF16), wl[:, cv_lo:cv_hi].astype(BF16), wl[:, sg_lo:sg_hi].astype(BF16),
            wl[:, gt_lo:gt_hi].astype(BF16),
            row(gla_norm_g[l]), w_o_gla[l].astype(BF16),
            jnp.pad(conv_w[l], ((0, 1), (0, 0))), row(conv_b[l]), row(conv_ln_g[l]), row(conv_ln_b[l]),
            w_o_conv[l].astype(BF16),
            row(sgu_ln_g[l]), row(sgu_ln_b[l]), sgu_ws[l].astype(BF16), sgu_bias, w_o_sgu[l].astype(BF16),
            w_out[l].astype(BF16),
        )
        w1 = w_ffn_in[l].astype(BF16)
        w2 = w_ffn_out[l].astype(BF16)
        g1, g2 = row(norm1_g[l]), row(norm2_g[l])
        mod_lat = ada[l, 0:bsz]
        mod_ctx = jnp.broadcast_to(ada[l, bsz:bsz + 1], (bsz, 8, d))

        qc, kc, vc, lgc, obc, s_b = _gla_bwd(h_ctx, mod_ctx, g1, w_qkva, aup, ab, zero_state, n_tok=ctx_len)
        onc, s_f = _gla_fwd(qc, kc, vc, lgc, obc, zero_state, n_tok=ctx_len)
        if l < depth - 1:
            h_ctx = _mixer(h_ctx, onc, mod_ctx, g1, mix_w, n_tok=ctx_len, seg=ctx_len)
            h_ctx = _ffn(h_ctx, mod_ctx, g2, w1, w2, n_tok=ctx_len)

        q, k, v, lgf, ob, _ = _gla_bwd(h_lat, mod_lat, g1, w_qkva, aup, ab, s_b, n_tok=t_scan)
        on, _ = _gla_fwd(q, k, v, lgf, ob, s_f, n_tok=t_scan)
        h_lat = _mixer(h_lat, on, mod_lat, g1, mix_w, n_tok=t_mix, seg=GRID_W)
        h_lat = _ffn(h_lat, mod_lat, g2, w1, w2, row(final_g) if l == depth - 1 else None, n_tok=t_lat)
    return h_lat
```
